```python
import jax, jax.numpy as jnp
from jax import lax
import numpy as np

D_MODEL = 1024
BATCH = 8
SEQ = 2048
DEPTH = 1

HEAD_DIM = 64
N_HEADS_A = D_MODEL // (2 * HEAD_DIM)
N_HEADS_B = D_MODEL // (2 * HEAD_DIM)
WIDTH_A = N_HEADS_A * HEAD_DIM
WIDTH_B = N_HEADS_B * HEAD_DIM
MIX_WIDTH = WIDTH_A + WIDTH_B
IN_WIDTH = 2 * WIDTH_A + 3 * WIDTH_B
CHUNK = 128
MOBA_BLOCK = 256
MOBA_TOPK = 3
Q_BLOCK = 128
D_FF = -(-8 * D_MODEL // (3 * 256)) * 256
EPS = 1e-6

kernel_name = "hymba_sgu_moba_hybrid_layer"


def rms_norm(x, g):
    xf = x.astype(jnp.float32)
    y = xf * lax.rsqrt(jnp.mean(xf * xf, axis=-1, keepdims=True) + EPS)
    return (y * g.astype(jnp.float32)).astype(x.dtype)


def layer_norm(x, g, b):
    xf = x.astype(jnp.float32)
    mu = jnp.mean(xf, axis=-1, keepdims=True)
    var = jnp.mean(jnp.square(xf - mu), axis=-1, keepdims=True)
    y = (xf - mu) * lax.rsqrt(var + EPS)
    return (y * g.astype(jnp.float32) + b.astype(jnp.float32)).astype(x.dtype)


def chunked_sgu(u, v, ln_g, ln_b, w_s, b_s):
    B, S, _ = u.shape
    u = jax.nn.gelu(u)
    v = jax.nn.gelu(v).reshape(B, S, N_HEADS_A, HEAD_DIM)
    v = layer_norm(v, ln_g.reshape(N_HEADS_A, HEAD_DIM), ln_b.reshape(N_HEADS_A, HEAD_DIM))
    v = v.reshape(B, S // CHUNK, CHUNK, N_HEADS_A, HEAD_DIM)
    causal = jnp.tril(jnp.ones((CHUNK, CHUNK), dtype=bool))
    w = jnp.where(causal[None], w_s, 0)
    mixed = jnp.einsum('hts,bcshd->bcthd', w, v) + b_s.T[None, None, :, :, None]
    return u * mixed.reshape(B, S, WIDTH_A)


def to_qblocks(a):
    B, H, S = a.shape[:3]
    a = a.reshape(B, H, S // Q_BLOCK, Q_BLOCK, *a.shape[3:])
    return jnp.moveaxis(a, 2, 0)


def moba_attention(q, k, v, q_g, k_g):
    B, S, _ = q.shape
    H = N_HEADS_B
    q = rms_norm(q.reshape(B, S, H, HEAD_DIM), q_g).transpose(0, 2, 1, 3)
    k = rms_norm(k.reshape(B, S, H, HEAD_DIM), k_g).transpose(0, 2, 1, 3)
    v = v.reshape(B, S, H, HEAD_DIM).transpose(0, 2, 1, 3)
    nb = -(-S // MOBA_BLOCK)
    pad = nb * MOBA_BLOCK - S
    padw = ((0, 0), (0, 0), (0, pad), (0, 0))
    kb = jnp.pad(k, padw).reshape(B, H, nb, MOBA_BLOCK, HEAD_DIM)
    vb = jnp.pad(v, padw).reshape(B, H, nb, MOBA_BLOCK, HEAD_DIM)
    k_mean = jnp.mean(kb, axis=3)

    q_blk = jnp.arange(S) // MOBA_BLOCK
    past = jnp.arange(nb)[None, :] < q_blk[:, None]
    gate = jnp.einsum('bhsd,bhnd->bhsn', q, k_mean).astype(jnp.float32)
    gate = jnp.where(past[None, None], gate, -jnp.inf)
    topk = min(MOBA_TOPK, nb)
    g_vals, g_idx = lax.top_k(gate, topk)
    g_valid = jnp.isfinite(g_vals)

    scale = HEAD_DIM ** -0.5
    nq = S // Q_BLOCK
    bi = jnp.arange(B)[:, None, None, None]
    hi = jnp.arange(H)[None, :, None, None]
    sel_len = topk * MOBA_BLOCK

    def block_fn(args):
        q_c, idx_c, valid_c, c = args
        q0 = c * Q_BLOCK
        own = q0 // MOBA_BLOCK
        k_sel = kb[bi, hi, idx_c]
        v_sel = vb[bi, hi, idx_c]
        s_sel = jnp.einsum('bhqd,bhqjkd->bhqjk', q_c, k_sel).astype(jnp.float32) * scale
        s_sel = jnp.where(valid_c[..., None], s_sel, -jnp.inf).reshape(B, H, Q_BLOCK, sel_len)
        k_own = lax.dynamic_index_in_dim(kb, own, axis=2, keepdims=False)
        v_own = lax.dynamic_index_in_dim(vb, own, axis=2, keepdims=False)
        s_own = jnp.einsum('bhqd,bhkd->bhqk', q_c, k_own).astype(jnp.float32) * scale
        q_pos = q0 + jnp.arange(Q_BLOCK)
        k_pos = own * MOBA_BLOCK + jnp.arange(MOBA_BLOCK)
        s_own = jnp.where(k_pos[None, :] <= q_pos[:, None], s_own, -jnp.inf)
        p = jax.nn.softmax(jnp.concatenate([s_sel, s_own], axis=-1), axis=-1)
        p_sel = p[..., :sel_len].reshape(B, H, Q_BLOCK, topk, MOBA_BLOCK).astype(v.dtype)
        p_own = p[..., sel_len:].astype(v.dtype)
        return (jnp.einsum('bhqjk,bhqjkd->bhqd', p_sel, v_sel)
                + jnp.einsum('bhqk,bhkd->bhqd', p_own, v_own))

    outs = lax.map(block_fn, (to_qblocks(q), to_qblocks(g_idx), to_qblocks(g_valid),
                              jnp.arange(nq)))
    out = jnp.moveaxis(outs, 0, 2).reshape(B, H, S, HEAD_DIM)
    return out.transpose(0, 2, 1, 3).reshape(B, S, WIDTH_B)


def setup_inputs(seed: int = 0) -> dict:
    key = jax.random.key(seed)
    ks = jax.random.split(key, 17)
    f32 = jnp.float32
    nrm = lambda k, shape, s: jax.random.normal(k, shape, f32) * s
    gain = lambda k, n: 1.0 + 0.02 * jax.random.normal(k, (DEPTH, n), f32)
    return {
        "x": jax.random.normal(ks[0], (BATCH, SEQ, D_MODEL), f32),
        "norm1_g": gain(ks[1], D_MODEL),
        "w_in": nrm(ks[2], (DEPTH, D_MODEL, IN_WIDTH), D_MODEL ** -0.5),
        "sgu_ln_g": gain(ks[3], WIDTH_A),
        "sgu_ln_b": nrm(ks[4], (DEPTH, WIDTH_A), 0.02),
        "sgu_w": nrm(ks[5], (DEPTH, N_HEADS_A, CHUNK, CHUNK), CHUNK ** -0.5),
        "sgu_b": 1.0 + nrm(ks[6], (DEPTH, N_HEADS_A, CHUNK), 0.02),
        "q_norm_g": gain(ks[7], HEAD_DIM),
        "k_norm_g": gain(ks[8], HEAD_DIM),
        "out_norm_a_g": gain(ks[9], WIDTH_A),
        "out_norm_b_g": gain(ks[10], WIDTH_B),
        "w_out": nrm(ks[11], (DEPTH, MIX_WIDTH, D_MODEL), MIX_WIDTH ** -0.5),
        "norm2_g": gain(ks[12], D_MODEL),
        "w_gate": nrm(ks[13], (DEPTH, D_MODEL, D_FF), D_MODEL ** -0.5),
        "w_up": nrm(ks[14], (DEPTH, D_MODEL, D_FF), D_MODEL ** -0.5),
        "w_down": nrm(ks[15], (DEPTH, D_FF, D_MODEL), D_FF ** -0.5),
    }


def reference(x, norm1_g, w_in, sgu_ln_g, sgu_ln_b, sgu_w, sgu_b, q_norm_g, k_norm_g,
              out_norm_a_g, out_norm_b_g, w_out, norm2_g, w_gate, w_up, w_down):
    B, S, _ = x.shape
    for l in range(DEPTH):
        h = rms_norm(x, norm1_g[l])
        proj = h @ w_in[l]
        u_a, v_a, q_b, k_b, v_b = jnp.split(
            proj, [WIDTH_A, 2 * WIDTH_A, 2 * WIDTH_A + WIDTH_B, 2 * WIDTH_A + 2 * WIDTH_B], axis=-1)
        y_a = chunked_sgu(u_a, v_a, sgu_ln_g[l], sgu_ln_b[l], sgu_w[l], sgu_b[l])
        y_b = moba_attention(q_b, k_b, v_b, q_norm_g[l], k_norm_g[l])
        y_a = rms_norm(y_a.reshape(B, S, N_HEADS_A, HEAD_DIM),
                       out_norm_a_g[l].reshape(N_HEADS_A, HEAD_DIM)).reshape(B, S, WIDTH_A)
        y_b = rms_norm(y_b.reshape(B, S, N_HEADS_B, HEAD_DIM),
                       out_norm_b_g[l].reshape(N_HEADS_B, HEAD_DIM)).reshape(B, S, WIDTH_B)
        x = x + jnp.concatenate([y_a, y_b], axis=-1) @ w_out[l]
        h = rms_norm(x, norm2_g[l])
        x = x + (jax.nn.silu(h @ w_gate[l]) * (h @ w_up[l])) @ w_down[l]
    return x
```

```python
import functools

import jax
import jax.numpy as jnp
from jax import lax
from jax.experimental import pallas as pl
from jax.experimental.pallas import tpu as pltpu

D_MODEL = 1024
HEAD_DIM = 64
N_HEADS = 8
WIDTH = N_HEADS * HEAD_DIM
CHUNK = 128
BLK = 256
TOPK = 3
D_FF = 2816
EPS = 1e-6
SCALE = HEAD_DIM ** -0.5

TM1 = 512
TM3 = 256
HEADS_PER_STEP = 2
PAIR = HEADS_PER_STEP * HEAD_DIM

VMEM_LIMIT_BYTES = 56 * 1024 * 1024

_NT = (((1,), (1,)), ((), ()))
_F32 = jnp.float32
_BF16 = jnp.bfloat16


def _head_rms(t, gain):
    ms = jnp.mean(t * t, axis=0, keepdims=True)
    return t * lax.rsqrt(ms + EPS) * gain


def _inproj_kernel(x_ref, g1_ref, wu_ref, wv_ref, wq_ref, wk_ref, wvb_ref,
                   lng_ref, lnb_ref, sw_ref, sb_ref, qg_ref, kg_ref, ag_ref,
                   ya_ref, qT_ref, k_ref, vT_ref, km_ref):
    si = pl.program_id(1)
    nblk = TM1 // BLK
    nchunk = TM1 // CHUNK

    x = x_ref[0]
    ms = jnp.mean(x * x, axis=-1, keepdims=True)
    h = (x * lax.rsqrt(ms + EPS) * g1_ref[...]).astype(_BF16)

    def proj_t(w_ref):
        return lax.dot_general(w_ref[...], h, _NT, preferred_element_type=_F32)

    u = jax.nn.gelu(proj_t(wu_ref))
    v = jax.nn.gelu(proj_t(wv_ref))
    row = lax.broadcasted_iota(jnp.int32, (CHUNK, CHUNK), 0)
    col = lax.broadcasted_iota(jnp.int32, (CHUNK, CHUNK), 1)
    causal = col <= row
    ya = []
    for hd in range(N_HEADS):
        sl = slice(hd * HEAD_DIM, (hd + 1) * HEAD_DIM)
        vh = v[sl, :]
        mu = jnp.mean(vh, axis=0, keepdims=True)
        vc = vh - mu
        var = jnp.mean(vc * vc, axis=0, keepdims=True)
        vn = vc * lax.rsqrt(var + EPS) * lng_ref[sl, :] + lnb_ref[sl, :]
        lhs = jnp.concatenate(
            [vn[:, c * CHUNK:(c + 1) * CHUNK] for c in range(nchunk)], axis=0).astype(_BF16)
        w = jnp.where(causal, sw_ref[hd], 0.0).astype(_BF16)
        mixed = lax.dot_general(lhs, w, _NT, preferred_element_type=_F32)
        mixed = mixed + sb_ref[hd:hd + 1, :]
        mixed = jnp.concatenate(
            [mixed[c * HEAD_DIM:(c + 1) * HEAD_DIM, :] for c in range(nchunk)], axis=1)
        ya.append(_head_rms(u[sl, :] * mixed, ag_ref[sl, :]))
    ya_ref[0] = jnp.concatenate(ya, axis=0).T.astype(_BF16)

    qt = proj_t(wq_ref)
    qn = jnp.concatenate(
        [_head_rms(qt[hd * HEAD_DIM:(hd + 1) * HEAD_DIM, :], qg_ref[...]) for hd in range(N_HEADS)],
        axis=0) * SCALE
    qn = qn.astype(_BF16)
    kt = proj_t(wk_ref)
    kn = jnp.concatenate(
        [_head_rms(kt[hd * HEAD_DIM:(hd + 1) * HEAD_DIM, :], kg_ref[...]) for hd in range(N_HEADS)],
        axis=0)
    k_tok = kn.T
    vt = proj_t(wvb_ref).astype(_BF16)
    for b in range(nblk):
        qT_ref[0, b] = qn[:, b * BLK:(b + 1) * BLK]
        vT_ref[0, b] = vt[:, b * BLK:(b + 1) * BLK]
        kb = k_tok[b * BLK:(b + 1) * BLK, :]
        k_ref[0, b] = kb.astype(_BF16)
        km_ref[0, pl.ds(si * nblk + b, 1), :] = jnp.mean(kb, axis=0, keepdims=True)


def _attn_kernel(qT_ref, k_ref, vT_ref, km_ref, bg_ref, yb_ref, sel_ref):
    j = pl.program_id(2)
    nb = k_ref.shape[1]
    qT = qT_ref[0, 0]
    km = km_ref[0]
    km_hi = km.astype(_BF16)
    km_lo = (km - km_hi.astype(_F32)).astype(_BF16)

    frow = lax.broadcasted_iota(jnp.int32, (PAIR, BLK), 0)
    blk_id = lax.broadcasted_iota(jnp.int32, (nb, BLK), 0)
    key_pos = lax.broadcasted_iota(jnp.int32, (BLK, BLK), 0)
    qry_pos = lax.broadcasted_iota(jnp.int32, (BLK, BLK), 1)
    past = blk_id < j

    outs = []
    for hh in range(HEADS_PER_STEP):
        in_head = (frow >= hh * HEAD_DIM) & (frow < (hh + 1) * HEAD_DIM)
        qm = jnp.where(in_head, qT, jnp.zeros_like(qT))
        vrows = slice(hh * HEAD_DIM, (hh + 1) * HEAD_DIM)

        gate = (jnp.dot(km_hi, qm, preferred_element_type=_F32)
                + jnp.dot(km_lo, qm, preferred_element_type=_F32))
        gate = jnp.where(past, gate, -jnp.inf)
        rank = jnp.zeros((nb, BLK), _F32)
        for m in range(nb):
            gm = gate[m:m + 1, :]
            ahead = (gm > gate) | ((gm == gate) & (blk_id > m))
            rank = rank + jnp.where(ahead, 1.0, 0.0)
        sel_ref[hh] = jnp.where(past & (rank < TOPK), 1.0, 0.0)

        s = jnp.dot(k_ref[0, j], qm, preferred_element_type=_F32)
        s = jnp.where(key_pos <= qry_pos, s, -jnp.inf)
        m0 = jnp.max(s, axis=0, keepdims=True)
        p = jnp.exp(s - m0)
        l0 = jnp.sum(p, axis=0, keepdims=True)
        acc0 = jnp.dot(vT_ref[0, j, vrows, :], p.astype(_BF16), preferred_element_type=_F32)

        def body(n, carry, hh=hh, qm=qm, vrows=vrows):
            m_i, l_i, acc = carry
            s = jnp.dot(k_ref[0, n], qm, preferred_element_type=_F32)
            keep = sel_ref[hh, pl.ds(n, 1), :] > 0.5
            s = jnp.where(keep, s, -jnp.inf)
            m_new = jnp.maximum(m_i, jnp.max(s, axis=0, keepdims=True))
            alpha = jnp.exp(m_i - m_new)
            p = jnp.exp(s - m_new)
            l_new = alpha * l_i + jnp.sum(p, axis=0, keepdims=True)
            pv = jnp.dot(vT_ref[0, n, vrows, :], p.astype(_BF16), preferred_element_type=_F32)
            return m_new, l_new, alpha * acc + pv

        _, l_f, acc = lax.fori_loop(0, j, body, (m0, l0, acc0))
        o = acc / l_f
        outs.append(_head_rms(o, bg_ref[vrows, :]))
    yb_ref[0] = jnp.concatenate(outs, axis=0).T.astype(_BF16)


def _ffn_kernel(x_ref, ya_ref, yb_ref, wo_ref, g2_ref, wg_ref, wu_ref, wd_ref, o_ref):
    y = jnp.concatenate([ya_ref[...], yb_ref[...]], axis=1)
    x1 = x_ref[...] + jnp.dot(y, wo_ref[...], preferred_element_type=_F32)
    ms = jnp.mean(x1 * x1, axis=-1, keepdims=True)
    h = (x1 * lax.rsqrt(ms + EPS) * g2_ref[...]).astype(_BF16)
    g = jnp.dot(h, wg_ref[...], preferred_element_type=_F32)
    u = jnp.dot(h, wu_ref[...], preferred_element_type=_F32)
    a = (jax.nn.silu(g) * u).astype(_BF16)
    o_ref[...] = x1 + jnp.dot(a, wd_ref[...], preferred_element_type=_F32)


def _full(shape):
    return pl.BlockSpec(shape, lambda *_: (0,) * len(shape))


def _layer(x, norm1_g, w_in, sgu_ln_g, sgu_ln_b, sgu_w, sgu_b, q_norm_g, k_norm_g,
           out_norm_a_g, out_norm_b_g, w_out, norm2_g, w_gate, w_up, w_down):
    B, S, D = x.shape
    nb = S // BLK
    T = B * S
    assert D == D_MODEL and S % TM1 == 0 and T % TM3 == 0

    w_in_t = w_in.astype(_BF16).T
    w_parts = [w_in_t[i * WIDTH:(i + 1) * WIDTH] for i in range(5)]
    col = lambda a: a.reshape(-1, 1).astype(_F32)
    qg = col(q_norm_g)
    kg = col(k_norm_g)

    ya, qT, k, vT, km = pl.pallas_call(
        _inproj_kernel,
        grid=(B, S // TM1),
        in_specs=[
            pl.BlockSpec((1, TM1, D), lambda b, s: (b, s, 0)),
            _full((1, D)),
            *[_full((WIDTH, D))] * 5,
            _full((WIDTH, 1)), _full((WIDTH, 1)),
            _full((N_HEADS, CHUNK, CHUNK)), _full((N_HEADS, CHUNK)),
            _full((HEAD_DIM, 1)), _full((HEAD_DIM, 1)), _full((WIDTH, 1)),
        ],
        out_specs=[
            pl.BlockSpec((1, TM1, WIDTH), lambda b, s: (b, s, 0)),
            pl.BlockSpec((1, TM1 // BLK, WIDTH, BLK), lambda b, s: (b, s, 0, 0)),
            pl.BlockSpec((1, TM1 // BLK, BLK, WIDTH), lambda b, s: (b, s, 0, 0)),
            pl.BlockSpec((1, TM1 // BLK, WIDTH, BLK), lambda b, s: (b, s, 0, 0)),
            pl.BlockSpec((1, nb, WIDTH), lambda b, s: (b, 0, 0)),
        ],
        out_shape=[
            jax.ShapeDtypeStruct((B, S, WIDTH), _BF16),
            jax.ShapeDtypeStruct((B, nb, WIDTH, BLK), _BF16),
            jax.ShapeDtypeStruct((B, nb, BLK, WIDTH), _BF16),
            jax.ShapeDtypeStruct((B, nb, WIDTH, BLK), _BF16),
            jax.ShapeDtypeStruct((B, nb, WIDTH), _F32),
        ],
        compiler_params=pltpu.CompilerParams(
            dimension_semantics=("parallel", "arbitrary"),
            vmem_limit_bytes=VMEM_LIMIT_BYTES),
        name="inproj_sgu",
    )(x, norm1_g.reshape(1, D), *w_parts, col(sgu_ln_g), col(sgu_ln_b), sgu_w, sgu_b,
      qg, kg, col(out_norm_a_g))

    n_pair = WIDTH // PAIR
    yb = pl.pallas_call(
        _attn_kernel,
        grid=(B, n_pair, nb),
        in_specs=[
            pl.BlockSpec((1, 1, PAIR, BLK), lambda b, p, j: (b, j, p, 0)),
            pl.BlockSpec((1, nb, BLK, PAIR), lambda b, p, j: (b, 0, 0, p)),
            pl.BlockSpec((1, nb, PAIR, BLK), lambda b, p, j: (b, 0, p, 0)),
            pl.BlockSpec((1, nb, PAIR), lambda b, p, j: (b, 0, p)),
            pl.BlockSpec((PAIR, 1), lambda b, p, j: (p, 0)),
        ],
        out_specs=pl.BlockSpec((1, BLK, PAIR), lambda b, p, j: (b, j, p)),
        out_shape=jax.ShapeDtypeStruct((B, S, WIDTH), _BF16),
        scratch_shapes=[pltpu.VMEM((HEADS_PER_STEP, nb, BLK), _F32)],
        compiler_params=pltpu.CompilerParams(
            dimension_semantics=("parallel", "parallel", "arbitrary"),
            vmem_limit_bytes=VMEM_LIMIT_BYTES),
        name="moba_attn",
    )(qT, k, vT, km, col(out_norm_b_g))

    out = pl.pallas_call(
        _ffn_kernel,
        grid=(T // TM3,),
        in_specs=[
            pl.BlockSpec((TM3, D), lambda i: (i, 0)),
            pl.BlockSpec((TM3, WIDTH), lambda i: (i, 0)),
            pl.BlockSpec((TM3, WIDTH), lambda i: (i, 0)),
            _full((2 * WIDTH, D)),
            _full((1, D)),
            _full((D, D_FF)), _full((D, D_FF)), _full((D_FF, D)),
        ],
        out_specs=pl.BlockSpec((TM3, D), lambda i: (i, 0)),
        out_shape=jax.ShapeDtypeStruct((T, D), _F32),
        compiler_params=pltpu.CompilerParams(
            dimension_semantics=("parallel",),
            vmem_limit_bytes=VMEM_LIMIT_BYTES),
        name="outproj_ffn",
    )(x.reshape(T, D), ya.reshape(T, WIDTH), yb.reshape(T, WIDTH), w_out.astype(_BF16),
      norm2_g.reshape(1, D), w_gate.astype(_BF16), w_up.astype(_BF16), w_down.astype(_BF16))
    return out.reshape(B, S, D)


def kernel(x, norm1_g, w_in, sgu_ln_g, sgu_ln_b, sgu_w, sgu_b, q_norm_g, k_norm_g,
           out_norm_a_g, out_norm_b_g, w_out, norm2_g, w_gate, w_up, w_down):
    depth = norm1_g.shape[0]
    for l in range(depth):
        x = _layer(x, norm1_g[l], w_in[l], sgu_ln_g[l], sgu_ln_b[l], sgu_w[l], sgu_b[l],
                   q_norm_g[l], k_norm_g[l], out_norm_a_g[l], out_norm_b_g[l], w_out[l],
                   norm2_g[l], w_gate[l], w_up[l], w_down[l])
    return x
```

```python
import functools

import jax
import jax.numpy as jnp
from jax import lax
from jax.experimental import pallas as pl
from jax.experimental.pallas import tpu as pltpu

D_MODEL = 1024
HEAD_DIM = 64
N_HEADS = 8
WIDTH = N_HEADS * HEAD_DIM
CHUNK = 128
BLK = 256
TOPK = 3
D_FF = 2816
EPS = 1e-6
SCALE = HEAD_DIM ** -0.5

TM1 = 512
TM3 = 256
HEADS_PER_STEP = 2
PAIR = HEADS_PER_STEP * HEAD_DIM

VMEM_LIMIT_BYTES = 56 * 1024 * 1024

_NT = (((1,), (1,)), ((), ()))
_F32 = jnp.float32
_BF16 = jnp.bfloat16


def _head_rms(t, gain):
    ms = jnp.mean(t * t, axis=0, keepdims=True)
    return t * lax.rsqrt(ms + EPS) * gain


def _inproj_kernel(x_ref, g1_ref, wu_ref, wv_ref, wq_ref, wk_ref, wvb_ref,
                   lng_ref, lnb_ref, sw_ref, sb_ref, qg_ref, kg_ref, ag_ref,
                   ya_ref, qT_ref, k_ref, vT_ref, km_ref):
    si = pl.program_id(1)
    nblk = TM1 // BLK
    nchunk = TM1 // CHUNK

    x = x_ref[0]
    ms = jnp.mean(x * x, axis=-1, keepdims=True)
    h = (x * lax.rsqrt(ms + EPS) * g1_ref[...]).astype(_BF16)

    def proj_t(w_ref):
        return lax.dot_general(w_ref[...], h, _NT, preferred_element_type=_F32)

    u = jax.nn.gelu(proj_t(wu_ref))
    v = jax.nn.gelu(proj_t(wv_ref))
    row = lax.broadcasted_iota(jnp.int32, (CHUNK, CHUNK), 0)
    col = lax.broadcasted_iota(jnp.int32, (CHUNK, CHUNK), 1)
    causal = col <= row
    ya = []
    for hd in range(N_HEADS):
        sl = slice(hd * HEAD_DIM, (hd + 1) * HEAD_DIM)
        vh = v[sl, :]
        mu = jnp.mean(vh, axis=0, keepdims=True)
        vc = vh - mu
        var = jnp.mean(vc * vc, axis=0, keepdims=True)
        vn = vc * lax.rsqrt(var + EPS) * lng_ref[sl, :] + lnb_ref[sl, :]
        lhs = jnp.concatenate(
            [vn[:, c * CHUNK:(c + 1) * CHUNK] for c in range(nchunk)], axis=0).astype(_BF16)
        w = jnp.where(causal, sw_ref[hd], 0.0).astype(_BF16)
        mixed = lax.dot_general(lhs, w, _NT, preferred_element_type=_F32)
        mixed = mixed + sb_ref[hd:hd + 1, :]
        mixed = jnp.concatenate(
            [mixed[c * HEAD_DIM:(c + 1) * HEAD_DIM, :] for c in range(nchunk)], axis=1)
        ya.append(_head_rms(u[sl, :] * mixed, ag_ref[sl, :]))
    ya_ref[0] = jnp.concatenate(ya, axis=0).T.astype(_BF16)

    qt = proj_t(wq_ref)
    qn = jnp.concatenate(
        [_head_rms(qt[hd * HEAD_DIM:(hd + 1) * HEAD_DIM, :], qg_ref[...]) for hd in range(N_HEADS)],
        axis=0) * SCALE
    qn = qn.astype(_BF16)
    kt = proj_t(wk_ref)
    kn = jnp.concatenate(
        [_head_rms(kt[hd * HEAD_DIM:(hd + 1) * HEAD_DIM, :], kg_ref[...]) for hd in range(N_HEADS)],
        axis=0)
    k_tok = kn.T
    qT_ref[0] = qn
    vT_ref[0] = proj_t(wvb_ref).astype(_BF16)
    k_ref[0] = k_tok.astype(_BF16)
    for b in range(nblk):
        kb = k_tok[b * BLK:(b + 1) * BLK, :]
        km_ref[0, pl.ds(si * nblk + b, 1), :] = jnp.mean(kb, axis=0, keepdims=True)


def _attn_tile(jj, qT_ref, k_ref, vT_ref, km_ref, bg_ref, yb_ref):
    nkeys = (jj + 1) * BLK
    qT = qT_ref[0]
    frow = lax.broadcasted_iota(jnp.int32, (PAIR, BLK), 0)
    key_pos = lax.broadcasted_iota(jnp.int32, (BLK, BLK), 0)
    qry_pos = lax.broadcasted_iota(jnp.int32, (BLK, BLK), 1)
    if jj > 0:
        km = km_ref[0, 0:jj, :]
        km_hi = km.astype(_BF16)
        km_lo = (km - km_hi.astype(_F32)).astype(_BF16)
        blk_id = lax.broadcasted_iota(jnp.int32, (jj, BLK), 0)

    heads = range(HEADS_PER_STEP)
    qms, scores = [], []
    for hh in heads:
        in_head = (frow >= hh * HEAD_DIM) & (frow < (hh + 1) * HEAD_DIM)
        qm = jnp.where(in_head, qT, jnp.zeros_like(qT))
        qms.append(qm)
        scores.append(jnp.dot(k_ref[0, 0:nkeys, :], qm, preferred_element_type=_F32))

    probs, denoms = [], []
    for hh in heads:
        qm, s = qms[hh], scores[hh]
        parts = []
        if jj > TOPK:
            gate = (jnp.dot(km_hi, qm, preferred_element_type=_F32)
                    + jnp.dot(km_lo, qm, preferred_element_type=_F32))
            rank = jnp.zeros((jj, BLK), _F32)
            for m in range(jj):
                gm = gate[m:m + 1, :]
                ahead = (gm > gate) | ((gm == gate) & (blk_id > m))
                rank = rank + jnp.where(ahead, 1.0, 0.0)
            bias = jnp.where(rank < TOPK, 0.0, -jnp.inf)
            for n in range(jj):
                parts.append(s[n * BLK:(n + 1) * BLK, :] + bias[n:n + 1, :])
        else:
            for n in range(jj):
                parts.append(s[n * BLK:(n + 1) * BLK, :])
        parts.append(jnp.where(key_pos <= qry_pos, s[jj * BLK:nkeys, :], -jnp.inf))
        s = jnp.concatenate(parts, axis=0) if jj > 0 else parts[0]

        m = jnp.max(s, axis=0, keepdims=True)
        p = jnp.exp(s - m)
        denoms.append(jnp.sum(p, axis=0, keepdims=True))
        probs.append(p.astype(_BF16))

    outs = []
    for hh in heads:
        vrows = slice(hh * HEAD_DIM, (hh + 1) * HEAD_DIM)
        o = jnp.dot(vT_ref[0, vrows, 0:nkeys], probs[hh], preferred_element_type=_F32)
        outs.append(_head_rms(o / denoms[hh], bg_ref[vrows, :]))
    yb_ref[0] = jnp.concatenate(outs, axis=0).T.astype(_BF16)


def _attn_kernel(qT_ref, k_ref, vT_ref, km_ref, bg_ref, yb_ref):
    j = pl.program_id(2)
    for jj in range(k_ref.shape[1] // BLK):
        pl.when(j == jj)(functools.partial(
            _attn_tile, jj, qT_ref, k_ref, vT_ref, km_ref, bg_ref, yb_ref))


def _ffn_kernel(x_ref, ya_ref, yb_ref, wo_ref, g2_ref, wg_ref, wu_ref, wd_ref, o_ref):
    y = jnp.concatenate([ya_ref[...], yb_ref[...]], axis=1)
    x1 = x_ref[...] + jnp.dot(y, wo_ref[...], preferred_element_type=_F32)
    ms = jnp.mean(x1 * x1, axis=-1, keepdims=True)
    h = (x1 * lax.rsqrt(ms + EPS) * g2_ref[...]).astype(_BF16)
    g = jnp.dot(h, wg_ref[...], preferred_element_type=_F32)
    u = jnp.dot(h, wu_ref[...], preferred_element_type=_F32)
    a = (jax.nn.silu(g) * u).astype(_BF16)
    o_ref[...] = x1 + jnp.dot(a, wd_ref[...], preferred_element_type=_F32)


def _full(shape):
    return pl.BlockSpec(shape, lambda *_: (0,) * len(shape))


def _layer(x, norm1_g, w_in, sgu_ln_g, sgu_ln_b, sgu_w, sgu_b, q_norm_g, k_norm_g,
           out_norm_a_g, out_norm_b_g, w_out, norm2_g, w_gate, w_up, w_down):
    B, S, D = x.shape
    nb = S // BLK
    T = B * S
    assert D == D_MODEL and S % TM1 == 0 and T % TM3 == 0

    w_in_t = w_in.astype(_BF16).T
    w_parts = [w_in_t[i * WIDTH:(i + 1) * WIDTH] for i in range(5)]
    col = lambda a: a.reshape(-1, 1).astype(_F32)
    qg = col(q_norm_g)
    kg = col(k_norm_g)

    ya, qT, k, vT, km = pl.pallas_call(
        _inproj_kernel,
        grid=(B, S // TM1),
        in_specs=[
            pl.BlockSpec((1, TM1, D), lambda b, s: (b, s, 0)),
            _full((1, D)),
            *[_full((WIDTH, D))] * 5,
            _full((WIDTH, 1)), _full((WIDTH, 1)),
            _full((N_HEADS, CHUNK, CHUNK)), _full((N_HEADS, CHUNK)),
            _full((HEAD_DIM, 1)), _full((HEAD_DIM, 1)), _full((WIDTH, 1)),
        ],
        out_specs=[
            pl.BlockSpec((1, TM1, WIDTH), lambda b, s: (b, s, 0)),
            pl.BlockSpec((1, WIDTH, TM1), lambda b, s: (b, 0, s)),
            pl.BlockSpec((1, TM1, WIDTH), lambda b, s: (b, s, 0)),
            pl.BlockSpec((1, WIDTH, TM1), lambda b, s: (b, 0, s)),
            pl.BlockSpec((1, nb, WIDTH), lambda b, s: (b, 0, 0)),
        ],
        out_shape=[
            jax.ShapeDtypeStruct((B, S, WIDTH), _BF16),
            jax.ShapeDtypeStruct((B, WIDTH, S), _BF16),
            jax.ShapeDtypeStruct((B, S, WIDTH), _BF16),
            jax.ShapeDtypeStruct((B, WIDTH, S), _BF16),
            jax.ShapeDtypeStruct((B, nb, WIDTH), _F32),
        ],
        compiler_params=pltpu.CompilerParams(
            dimension_semantics=("parallel", "arbitrary"),
            vmem_limit_bytes=VMEM_LIMIT_BYTES),
        name="inproj_sgu",
    )(x, norm1_g.reshape(1, D), *w_parts, col(sgu_ln_g), col(sgu_ln_b), sgu_w, sgu_b,
      qg, kg, col(out_norm_a_g))

    n_pair = WIDTH // PAIR
    yb = pl.pallas_call(
        _attn_kernel,
        grid=(B, n_pair, nb),
        in_specs=[
            pl.BlockSpec((1, PAIR, BLK), lambda b, p, j: (b, p, j)),
            pl.BlockSpec((1, S, PAIR), lambda b, p, j: (b, 0, p)),
            pl.BlockSpec((1, PAIR, S), lambda b, p, j: (b, p, 0)),
            pl.BlockSpec((1, nb, PAIR), lambda b, p, j: (b, 0, p)),
            pl.BlockSpec((PAIR, 1), lambda b, p, j: (p, 0)),
        ],
        out_specs=pl.BlockSpec((1, BLK, PAIR), lambda b, p, j: (b, j, p)),
        out_shape=jax.ShapeDtypeStruct((B, S, WIDTH), _BF16),
        compiler_params=pltpu.CompilerParams(
            dimension_semantics=("parallel", "parallel", "arbitrary"),
            vmem_limit_bytes=VMEM_LIMIT_BYTES),
        name="moba_attn",
    )(qT, k, vT, km, col(out_norm_b_g))

    out = pl.pallas_call(
        _ffn_kernel,
        grid=(T // TM3,),
        in_specs=[
            pl.BlockSpec((TM3, D), lambda i: (i, 0)),
            pl.BlockSpec((TM3, WIDTH), lambda i: (i, 0)),
            pl.BlockSpec((TM3, WIDTH), lambda i: (i, 0)),
            _full((2 * WIDTH, D)),
            _full((1, D)),
            _full((D, D_FF)), _full((D, D_FF)), _full((D_FF, D)),
        ],
        out_specs=pl.BlockSpec((TM3, D), lambda i: (i, 0)),
        out_shape=jax.ShapeDtypeStruct((T, D), _F32),
        compiler_params=pltpu.CompilerParams(
            dimension_semantics=("parallel",),
            vmem_limit_bytes=VMEM_LIMIT_BYTES),
        name="outproj_ffn",
    )(x.reshape(T, D), ya.reshape(T, WIDTH), yb.reshape(T, WIDTH), w_out.astype(_BF16),
      norm2_g.reshape(1, D), w_gate.astype(_BF16), w_up.astype(_BF16), w_down.astype(_BF16))
    return out.reshape(B, S, D)


def kernel(x, norm1_g, w_in, sgu_ln_g, sgu_ln_b, sgu_w, sgu_b, q_norm_g, k_norm_g,
           out_norm_a_g, out_norm_b_g, w_out, norm2_g, w_gate, w_up, w_down):
    depth = norm1_g.shape[0]
    for l in range(depth):
        x = _layer(x, norm1_g[l], w_in[l], sgu_ln_g[l], sgu_ln_b[l], sgu_w[l], sgu_b[l],
                   q_norm_g[l], k_norm_g[l], out_norm_a_g[l], out_norm_b_g[l], w_out[l],
                   norm2_g[l], w_gate[l], w_up[l], w_down[l])
    return x
```

```python
import jax
import jax.numpy as jnp
from jax import lax
from jax.experimental import pallas as pl
from jax.experimental.pallas import tpu as pltpu

D_MODEL = 1024
HEAD_DIM = 64
N_HEADS = 8
WIDTH = N_HEADS * HEAD_DIM
CHUNK = 128
BLK = 256
TOPK = 3
D_FF = 2816
EPS = 1e-6
SCALE = HEAD_DIM ** -0.5

TM1 = 512
TM3 = 256
HEADS_PER_STEP = 2
PAIR = HEADS_PER_STEP * HEAD_DIM

VMEM_LIMIT_BYTES = 56 * 1024 * 1024

_NT = (((1,), (1,)), ((), ()))
_F32 = jnp.float32
_BF16 = jnp.bfloat16


def _head_rms(t, gain):
    ms = jnp.mean(t * t, axis=0, keepdims=True)
    return t * lax.rsqrt(ms + EPS) * gain


def _inproj_kernel(x_ref, g1_ref, wu_ref, wv_ref, wq_ref, wk_ref, wvb_ref,
                   lng_ref, lnb_ref, sw_ref, sb_ref, qg_ref, kg_ref, ag_ref,
                   ya_ref, qT_ref, k_ref, vT_ref, km_ref):
    si = pl.program_id(1)
    nblk = TM1 // BLK
    nchunk = TM1 // CHUNK

    x = x_ref[0]
    ms = jnp.mean(x * x, axis=-1, keepdims=True)
    h = (x * lax.rsqrt(ms + EPS) * g1_ref[...]).astype(_BF16)

    def proj_t(w_ref):
        return lax.dot_general(w_ref[...], h, _NT, preferred_element_type=_F32)

    u = jax.nn.gelu(proj_t(wu_ref))
    v = jax.nn.gelu(proj_t(wv_ref))
    row = lax.broadcasted_iota(jnp.int32, (CHUNK, CHUNK), 0)
    col = lax.broadcasted_iota(jnp.int32, (CHUNK, CHUNK), 1)
    causal = col <= row
    ya = []
    for hd in range(N_HEADS):
        sl = slice(hd * HEAD_DIM, (hd + 1) * HEAD_DIM)
        vh = v[sl, :]
        mu = jnp.mean(vh, axis=0, keepdims=True)
        vc = vh - mu
        var = jnp.mean(vc * vc, axis=0, keepdims=True)
        vn = vc * lax.rsqrt(var + EPS) * lng_ref[sl, :] + lnb_ref[sl, :]
        lhs = jnp.concatenate(
            [vn[:, c * CHUNK:(c + 1) * CHUNK] for c in range(nchunk)], axis=0).astype(_BF16)
        w = jnp.where(causal, sw_ref[hd], 0.0).astype(_BF16)
        mixed = lax.dot_general(lhs, w, _NT, preferred_element_type=_F32)
        mixed = mixed + sb_ref[hd:hd + 1, :]
        mixed = jnp.concatenate(
            [mixed[c * HEAD_DIM:(c + 1) * HEAD_DIM, :] for c in range(nchunk)], axis=1)
        ya.append(_head_rms(u[sl, :] * mixed, ag_ref[sl, :]))
    ya_ref[0] = jnp.concatenate(ya, axis=0).T.astype(_BF16)

    qt = proj_t(wq_ref)
    qn = jnp.concatenate(
        [_head_rms(qt[hd * HEAD_DIM:(hd + 1) * HEAD_DIM, :], qg_ref[...]) for hd in range(N_HEADS)],
        axis=0) * SCALE
    qn = qn.astype(_BF16)
    kt = proj_t(wk_ref)
    kn = jnp.concatenate(
        [_head_rms(kt[hd * HEAD_DIM:(hd + 1) * HEAD_DIM, :], kg_ref[...]) for hd in range(N_HEADS)],
        axis=0)
    k_tok = kn.T
    qT_ref[0] = qn
    vT_ref[0] = proj_t(wvb_ref).astype(_BF16)
    k_ref[0] = k_tok.astype(_BF16)
    for b in range(nblk):
        kb = k_tok[b * BLK:(b + 1) * BLK, :]
        km_ref[0, pl.ds(si * nblk + b, 1), :] = jnp.mean(kb, axis=0, keepdims=True)


def _attn_kernel(qT_ref, k_ref, vT_ref, km_ref, bg_ref, yb_ref):
    nb = k_ref.shape[1] // BLK
    frow = lax.broadcasted_iota(jnp.int32, (PAIR, BLK), 0)
    key_pos = lax.broadcasted_iota(jnp.int32, (BLK, BLK), 0)
    qry_pos = lax.broadcasted_iota(jnp.int32, (BLK, BLK), 1)
    causal_bias = jnp.where(key_pos <= qry_pos, 0.0, -jnp.inf)

    def scores(jj, hh):
        nkeys = (jj + 1) * BLK
        qT = qT_ref[0, :, jj * BLK:(jj + 1) * BLK]
        in_head = (frow >= hh * HEAD_DIM) & (frow < (hh + 1) * HEAD_DIM)
        qm = jnp.where(in_head, qT, jnp.zeros_like(qT))
        bias = None
        if jj > TOPK:
            km = km_ref[0, 0:jj, :]
            km_hi = km.astype(_BF16)
            km_lo = (km - km_hi.astype(_F32)).astype(_BF16)
            blk_id = lax.broadcasted_iota(jnp.int32, (jj, BLK), 0)
            gate = (jnp.dot(km_hi, qm, preferred_element_type=_F32)
                    + jnp.dot(km_lo, qm, preferred_element_type=_F32))
            rank = jnp.zeros((jj, BLK), _F32)
            for m in range(jj):
                gm = gate[m:m + 1, :]
                ahead = (gm > gate) | ((gm == gate) & (blk_id > m))
                rank = rank + jnp.where(ahead, 1.0, 0.0)
            bias = jnp.where(rank < TOPK, 0.0, -jnp.inf)
        s = jnp.dot(k_ref[0, 0:nkeys, :], qm, preferred_element_type=_F32)
        parts = []
        for n in range(jj):
            sn = s[n * BLK:(n + 1) * BLK, :]
            parts.append(sn if bias is None else sn + bias[n:n + 1, :])
        parts.append(s[jj * BLK:nkeys, :] + causal_bias)
        s = jnp.concatenate(parts, axis=0) if jj > 0 else parts[0]
        return s, jnp.max(s, axis=0, keepdims=True)

    tiles = {}

    def finish(jj, hh, s, m):
        p = jnp.exp(s - m)
        l = jnp.sum(p, axis=0, keepdims=True)
        vrows = slice(hh * HEAD_DIM, (hh + 1) * HEAD_DIM)
        o = jnp.dot(vT_ref[0, vrows, 0:(jj + 1) * BLK], p.astype(_BF16), preferred_element_type=_F32)
        tiles.setdefault(jj, []).append(_head_rms(o / l, bg_ref[vrows, :]))
        if len(tiles[jj]) == HEADS_PER_STEP:
            y = jnp.concatenate(tiles.pop(jj), axis=0)
            yb_ref[0, jj * BLK:(jj + 1) * BLK, :] = y.T.astype(_BF16)

    pending = None
    for jj in range(nb):
        for hh in range(HEADS_PER_STEP):
            s, m = scores(jj, hh)
            if pending is not None:
                finish(*pending)
            pending = (jj, hh, s, m)
    finish(*pending)


def _ffn_kernel(x_ref, ya_ref, yb_ref, wo_ref, g2_ref, wg_ref, wu_ref, wd_ref, o_ref):
    y = jnp.concatenate([ya_ref[...], yb_ref[...]], axis=1)
    x1 = x_ref[...] + jnp.dot(y, wo_ref[...], preferred_element_type=_F32)
    ms = jnp.mean(x1 * x1, axis=-1, keepdims=True)
    h = (x1 * lax.rsqrt(ms + EPS) * g2_ref[...]).astype(_BF16)
    g = jnp.dot(h, wg_ref[...], preferred_element_type=_F32)
    u = jnp.dot(h, wu_ref[...], preferred_element_type=_F32)
    a = (jax.nn.silu(g) * u).astype(_BF16)
    o_ref[...] = x1 + jnp.dot(a, wd_ref[...], preferred_element_type=_F32)


def _full(shape):
    return pl.BlockSpec(shape, lambda *_: (0,) * len(shape))


def _layer(x, norm1_g, w_in, sgu_ln_g, sgu_ln_b, sgu_w, sgu_b, q_norm_g, k_norm_g,
           out_norm_a_g, out_norm_b_g, w_out, norm2_g, w_gate, w_up, w_down):
    B, S, D = x.shape
    nb = S // BLK
    T = B * S
    assert D == D_MODEL and S % TM1 == 0 and T % TM3 == 0

    w_in_t = w_in.astype(_BF16).T
    w_parts = [w_in_t[i * WIDTH:(i + 1) * WIDTH] for i in range(5)]
    col = lambda a: a.reshape(-1, 1).astype(_F32)
    qg = col(q_norm_g)
    kg = col(k_norm_g)

    ya, qT, k, vT, km = pl.pallas_call(
        _inproj_kernel,
        grid=(B, S // TM1),
        in_specs=[
            pl.BlockSpec((1, TM1, D), lambda b, s: (b, s, 0)),
            _full((1, D)),
            *[_full((WIDTH, D))] * 5,
            _full((WIDTH, 1)), _full((WIDTH, 1)),
            _full((N_HEADS, CHUNK, CHUNK)), _full((N_HEADS, CHUNK)),
            _full((HEAD_DIM, 1)), _full((HEAD_DIM, 1)), _full((WIDTH, 1)),
        ],
        out_specs=[
            pl.BlockSpec((1, TM1, WIDTH), lambda b, s: (b, s, 0)),
            pl.BlockSpec((1, WIDTH, TM1), lambda b, s: (b, 0, s)),
            pl.BlockSpec((1, TM1, WIDTH), lambda b, s: (b, s, 0)),
            pl.BlockSpec((1, WIDTH, TM1), lambda b, s: (b, 0, s)),
            pl.BlockSpec((1, nb, WIDTH), lambda b, s: (b, 0, 0)),
        ],
        out_shape=[
            jax.ShapeDtypeStruct((B, S, WIDTH), _BF16),
            jax.ShapeDtypeStruct((B, WIDTH, S), _BF16),
            jax.ShapeDtypeStruct((B, S, WIDTH), _BF16),
            jax.ShapeDtypeStruct((B, WIDTH, S), _BF16),
            jax.ShapeDtypeStruct((B, nb, WIDTH), _F32),
        ],
        compiler_params=pltpu.CompilerParams(
            dimension_semantics=("parallel", "arbitrary"),
            vmem_limit_bytes=VMEM_LIMIT_BYTES),
        name="inproj_sgu",
    )(x, norm1_g.reshape(1, D), *w_parts, col(sgu_ln_g), col(sgu_ln_b), sgu_w, sgu_b,
      qg, kg, col(out_norm_a_g))

    n_pair = WIDTH // PAIR
    yb = pl.pallas_call(
        _attn_kernel,
        grid=(B, n_pair),
        in_specs=[
            pl.BlockSpec((1, PAIR, S), lambda b, p: (b, p, 0)),
            pl.BlockSpec((1, S, PAIR), lambda b, p: (b, 0, p)),
            pl.BlockSpec((1, PAIR, S), lambda b, p: (b, p, 0)),
            pl.BlockSpec((1, nb, PAIR), lambda b, p: (b, 0, p)),
            pl.BlockSpec((PAIR, 1), lambda b, p: (p, 0)),
        ],
        out_specs=pl.BlockSpec((1, S, PAIR), lambda b, p: (b, 0, p)),
        out_shape=jax.ShapeDtypeStruct((B, S, WIDTH), _BF16),
        compiler_params=pltpu.CompilerParams(
            dimension_semantics=("parallel", "parallel"),
            vmem_limit_bytes=VMEM_LIMIT_BYTES),
        name="moba_attn",
    )(qT, k, vT, km, col(out_norm_b_g))

    out = pl.pallas_call(
        _ffn_kernel,
        grid=(T // TM3,),
        in_specs=[
            pl.BlockSpec((TM3, D), lambda i: (i, 0)),
            pl.BlockSpec((TM3, WIDTH), lambda i: (i, 0)),
            pl.BlockSpec((TM3, WIDTH), lambda i: (i, 0)),
            _full((2 * WIDTH, D)),
            _full((1, D)),
            _full((D, D_FF)), _full((D, D_FF)), _full((D_FF, D)),
        ],
        out_specs=pl.BlockSpec((TM3, D), lambda i: (i, 0)),
        out_shape=jax.ShapeDtypeStruct((T, D), _F32),
        compiler_params=pltpu.CompilerParams(
            dimension_semantics=("parallel",),
            vmem_limit_bytes=VMEM_LIMIT_BYTES),
        name="outproj_ffn",
    )(x.reshape(T, D), ya.reshape(T, WIDTH), yb.reshape(T, WIDTH), w_out.astype(_BF16),
      norm2_g.reshape(1, D), w_gate.astype(_BF16), w_up.astype(_BF16), w_down.astype(_BF16))
    return out.reshape(B, S, D)


def kernel(x, norm1_g, w_in, sgu_ln_g, sgu_ln_b, sgu_w, sgu_b, q_norm_g, k_norm_g,
           out_norm_a_g, out_norm_b_g, w_out, norm2_g, w_gate, w_up, w_down):
    depth = norm1_g.shape[0]
    for l in range(depth):
        x = _layer(x, norm1_g[l], w_in[l], sgu_ln_g[l], sgu_ln_b[l], sgu_w[l], sgu_b[l],
                   q_norm_g[l], k_norm_g[l], out_norm_a_g[l], out_norm_b_g[l], w_out[l],
                   norm2_g[l], w_gate[l], w_up[l], w_down[l])
    return x
```

```python
import jax
import jax.numpy as jnp
from jax import lax
from jax.experimental import pallas as pl
from jax.experimental.pallas import tpu as pltpu

D_MODEL = 1024
HEAD_DIM = 64
N_HEADS = 8
WIDTH = N_HEADS * HEAD_DIM
CHUNK = 128
BLK = 256
TOPK = 3
D_FF = 2816
EPS = 1e-6
Q_SCALE = HEAD_DIM ** -0.5 * 1.4426950408889634

TM1 = 512
TM3 = 512
HEADS_PER_STEP = 2
PAIR = HEADS_PER_STEP * HEAD_DIM
SCORES_AHEAD = 2

VMEM_LIMIT_BYTES = 56 * 1024 * 1024

_NT = (((1,), (1,)), ((), ()))
_F32 = jnp.float32
_BF16 = jnp.bfloat16


def _head_rms(t, gain):
    ms = jnp.mean(t * t, axis=0, keepdims=True)
    return t * lax.rsqrt(ms + EPS) * gain


def _inproj_kernel(x_ref, g1_ref, wu_ref, wv_ref, wq_ref, wk_ref, wvb_ref,
                   lng_ref, lnb_ref, sw_ref, sb_ref, qg_ref, kg_ref, ag_ref,
                   ya_ref, qT_ref, k_ref, vT_ref, km_ref):
    si = pl.program_id(1)
    nblk = TM1 // BLK
    nchunk = TM1 // CHUNK

    x = x_ref[0]
    ms = jnp.mean(x * x, axis=-1, keepdims=True)
    h = (x * lax.rsqrt(ms + EPS) * g1_ref[...]).astype(_BF16)

    def proj_t(w_ref):
        return lax.dot_general(w_ref[...], h, _NT, preferred_element_type=_F32)

    u = jax.nn.gelu(proj_t(wu_ref))
    v = jax.nn.gelu(proj_t(wv_ref))
    row = lax.broadcasted_iota(jnp.int32, (CHUNK, CHUNK), 0)
    col = lax.broadcasted_iota(jnp.int32, (CHUNK, CHUNK), 1)
    causal = col <= row
    ya = []
    for hd in range(N_HEADS):
        sl = slice(hd * HEAD_DIM, (hd + 1) * HEAD_DIM)
        vh = v[sl, :]
        mu = jnp.mean(vh, axis=0, keepdims=True)
        vc = vh - mu
        var = jnp.mean(vc * vc, axis=0, keepdims=True)
        vn = vc * lax.rsqrt(var + EPS) * lng_ref[sl, :] + lnb_ref[sl, :]
        lhs = jnp.concatenate(
            [vn[:, c * CHUNK:(c + 1) * CHUNK] for c in range(nchunk)], axis=0).astype(_BF16)
        w = jnp.where(causal, sw_ref[hd], 0.0).astype(_BF16)
        mixed = lax.dot_general(lhs, w, _NT, preferred_element_type=_F32)
        mixed = mixed + sb_ref[hd:hd + 1, :]
        mixed = jnp.concatenate(
            [mixed[c * HEAD_DIM:(c + 1) * HEAD_DIM, :] for c in range(nchunk)], axis=1)
        ya.append(_head_rms(u[sl, :] * mixed, ag_ref[sl, :]))
    ya_ref[0] = jnp.concatenate(ya, axis=0).T.astype(_BF16)

    qt = proj_t(wq_ref)
    qn = jnp.concatenate(
        [_head_rms(qt[hd * HEAD_DIM:(hd + 1) * HEAD_DIM, :], qg_ref[...]) for hd in range(N_HEADS)],
        axis=0) * Q_SCALE
    qn = qn.astype(_BF16)
    kt = proj_t(wk_ref)
    kn = jnp.concatenate(
        [_head_rms(kt[hd * HEAD_DIM:(hd + 1) * HEAD_DIM, :], kg_ref[...]) for hd in range(N_HEADS)],
        axis=0)
    k_tok = kn.T
    qT_ref[0] = qn
    vT_ref[0] = proj_t(wvb_ref).astype(_BF16)
    k_ref[0] = k_tok.astype(_BF16)
    for b in range(nblk):
        kb = k_tok[b * BLK:(b + 1) * BLK, :]
        km_ref[0, pl.ds(si * nblk + b, 1), :] = jnp.mean(kb, axis=0, keepdims=True)


def _attn_kernel(qT_ref, k_ref, vT_ref, km_ref, bg_ref, yb_ref):
    nb = k_ref.shape[1] // BLK
    frow = lax.broadcasted_iota(jnp.int32, (PAIR, BLK), 0)
    key_pos = lax.broadcasted_iota(jnp.int32, (BLK, BLK), 0)
    qry_pos = lax.broadcasted_iota(jnp.int32, (BLK, BLK), 1)
    causal_bias = jnp.where(key_pos <= qry_pos, 0.0, -jnp.inf)

    def scores(jj, hh):
        nkeys = (jj + 1) * BLK
        qT = qT_ref[0, :, jj * BLK:(jj + 1) * BLK]
        in_head = (frow >= hh * HEAD_DIM) & (frow < (hh + 1) * HEAD_DIM)
        qm = jnp.where(in_head, qT, jnp.zeros_like(qT))
        bias = None
        if jj > TOPK:
            km = km_ref[0, 0:jj, :]
            km_hi = km.astype(_BF16)
            km_lo = (km - km_hi.astype(_F32)).astype(_BF16)
            blk_id = lax.broadcasted_iota(jnp.int32, (jj, BLK), 0)
            gate = (jnp.dot(km_hi, qm, preferred_element_type=_F32)
                    + jnp.dot(km_lo, qm, preferred_element_type=_F32))
            rank = jnp.zeros((jj, BLK), _F32)
            for m in range(jj):
                gm = gate[m:m + 1, :]
                ahead = (gm > gate) | ((gm == gate) & (blk_id > m))
                rank = rank + jnp.where(ahead, 1.0, 0.0)
            bias = jnp.where(rank < TOPK, 0.0, -jnp.inf)
        s = jnp.dot(k_ref[0, 0:nkeys, :], qm, preferred_element_type=_F32)
        parts = []
        for n in range(jj):
            sn = s[n * BLK:(n + 1) * BLK, :]
            parts.append(sn if bias is None else sn + bias[n:n + 1, :])
        parts.append(s[jj * BLK:nkeys, :] + causal_bias)
        s = jnp.concatenate(parts, axis=0) if jj > 0 else parts[0]
        return s, jnp.max(s, axis=0, keepdims=True)

    tiles = {}

    def finish(jj, hh, s, m):
        p = jnp.exp2(s - m)
        l = jnp.sum(p, axis=0, keepdims=True)
        vrows = slice(hh * HEAD_DIM, (hh + 1) * HEAD_DIM)
        o = jnp.dot(vT_ref[0, vrows, 0:(jj + 1) * BLK], p.astype(_BF16), preferred_element_type=_F32)
        tiles.setdefault(jj, []).append(_head_rms(o / l, bg_ref[vrows, :]))
        if len(tiles[jj]) == HEADS_PER_STEP:
            y = jnp.concatenate(tiles.pop(jj), axis=0)
            yb_ref[0, jj * BLK:(jj + 1) * BLK, :] = y.T.astype(_BF16)

    pending = []
    for jj in range(nb):
        for hh in range(HEADS_PER_STEP):
            pending.append((jj, hh, *scores(jj, hh)))
            if len(pending) > SCORES_AHEAD:
                finish(*pending.pop(0))
    for job in pending:
        finish(*job)


def _ffn_kernel(x_ref, ya_ref, yb_ref, wo_ref, g2_ref, wg_ref, wu_ref, wd_ref, o_ref):
    y = jnp.concatenate([ya_ref[...], yb_ref[...]], axis=1)
    x1 = x_ref[...] + jnp.dot(y, wo_ref[...], preferred_element_type=_F32)
    ms = jnp.mean(x1 * x1, axis=-1, keepdims=True)
    h = (x1 * lax.rsqrt(ms + EPS) * g2_ref[...]).astype(_BF16)
    g = jnp.dot(h, wg_ref[...], preferred_element_type=_F32)
    u = jnp.dot(h, wu_ref[...], preferred_element_type=_F32)
    a = (jax.nn.silu(g) * u).astype(_BF16)
    o_ref[...] = x1 + jnp.dot(a, wd_ref[...], preferred_element_type=_F32)


def _full(shape):
    return pl.BlockSpec(shape, lambda *_: (0,) * len(shape), pipeline_mode=pl.Buffered(1))


def _layer(x, norm1_g, w_in, sgu_ln_g, sgu_ln_b, sgu_w, sgu_b, q_norm_g, k_norm_g,
           out_norm_a_g, out_norm_b_g, w_out, norm2_g, w_gate, w_up, w_down):
    B, S, D = x.shape
    nb = S // BLK
    T = B * S
    assert D == D_MODEL and S % TM1 == 0 and T % TM3 == 0

    w_in_t = w_in.astype(_BF16).T
    w_parts = [w_in_t[i * WIDTH:(i + 1) * WIDTH] for i in range(5)]
    col = lambda a: a.reshape(-1, 1).astype(_F32)
    qg = col(q_norm_g)
    kg = col(k_norm_g)

    ya, qT, k, vT, km = pl.pallas_call(
        _inproj_kernel,
        grid=(B, S // TM1),
        in_specs=[
            pl.BlockSpec((1, TM1, D), lambda b, s: (b, s, 0)),
            _full((1, D)),
            *[_full((WIDTH, D))] * 5,
            _full((WIDTH, 1)), _full((WIDTH, 1)),
            _full((N_HEADS, CHUNK, CHUNK)), _full((N_HEADS, CHUNK)),
            _full((HEAD_DIM, 1)), _full((HEAD_DIM, 1)), _full((WIDTH, 1)),
        ],
        out_specs=[
            pl.BlockSpec((1, TM1, WIDTH), lambda b, s: (b, s, 0)),
            pl.BlockSpec((1, WIDTH, TM1), lambda b, s: (b, 0, s)),
            pl.BlockSpec((1, TM1, WIDTH), lambda b, s: (b, s, 0)),
            pl.BlockSpec((1, WIDTH, TM1), lambda b, s: (b, 0, s)),
            pl.BlockSpec((1, nb, WIDTH), lambda b, s: (b, 0, 0)),
        ],
        out_shape=[
            jax.ShapeDtypeStruct((B, S, WIDTH), _BF16),
            jax.ShapeDtypeStruct((B, WIDTH, S), _BF16),
            jax.ShapeDtypeStruct((B, S, WIDTH), _BF16),
            jax.ShapeDtypeStruct((B, WIDTH, S), _BF16),
            jax.ShapeDtypeStruct((B, nb, WIDTH), _F32),
        ],
        compiler_params=pltpu.CompilerParams(
            dimension_semantics=("parallel", "arbitrary"),
            vmem_limit_bytes=VMEM_LIMIT_BYTES),
        name="inproj_sgu",
    )(x, norm1_g.reshape(1, D), *w_parts, col(sgu_ln_g), col(sgu_ln_b), sgu_w, sgu_b,
      qg, kg, col(out_norm_a_g))

    n_pair = WIDTH // PAIR
    yb = pl.pallas_call(
        _attn_kernel,
        grid=(B, n_pair),
        in_specs=[
            pl.BlockSpec((1, PAIR, S), lambda b, p: (b, p, 0)),
            pl.BlockSpec((1, S, PAIR), lambda b, p: (b, 0, p)),
            pl.BlockSpec((1, PAIR, S), lambda b, p: (b, p, 0)),
            pl.BlockSpec((1, nb, PAIR), lambda b, p: (b, 0, p)),
            pl.BlockSpec((PAIR, 1), lambda b, p: (p, 0)),
        ],
        out_specs=pl.BlockSpec((1, S, PAIR), lambda b, p: (b, 0, p)),
        out_shape=jax.ShapeDtypeStruct((B, S, WIDTH), _BF16),
        compiler_params=pltpu.CompilerParams(
            dimension_semantics=("parallel", "parallel"),
            vmem_limit_bytes=VMEM_LIMIT_BYTES),
        name="moba_attn",
    )(qT, k, vT, km, col(out_norm_b_g))

    out = pl.pallas_call(
        _ffn_kernel,
        grid=(T // TM3,),
        in_specs=[
            pl.BlockSpec((TM3, D), lambda i: (i, 0)),
            pl.BlockSpec((TM3, WIDTH), lambda i: (i, 0)),
            pl.BlockSpec((TM3, WIDTH), lambda i: (i, 0)),
            _full((2 * WIDTH, D)),
            _full((1, D)),
            _full((D, D_FF)), _full((D, D_FF)), _full((D_FF, D)),
        ],
        out_specs=pl.BlockSpec((TM3, D), lambda i: (i, 0)),
        out_shape=jax.ShapeDtypeStruct((T, D), _F32),
        compiler_params=pltpu.CompilerParams(
            dimension_semantics=("parallel",),
            vmem_limit_bytes=VMEM_LIMIT_BYTES),
        name="outproj_ffn",
    )(x.reshape(T, D), ya.reshape(T, WIDTH), yb.reshape(T, WIDTH), w_out.astype(_BF16),
      norm2_g.reshape(1, D), w_gate.astype(_BF16), w_up.astype(_BF16), w_down.astype(_BF16))
    return out.reshape(B, S, D)


def kernel(x, norm1_g, w_in, sgu_ln_g, sgu_ln_b, sgu_w, sgu_b, q_norm_g, k_norm_g,
           out_norm_a_g, out_norm_b_g, w_out, norm2_g, w_gate, w_up, w_down):
    depth = norm1_g.shape[0]
    for l in range(depth):
        x = _layer(x, norm1_g[l], w_in[l], sgu_ln_g[l], sgu_ln_b[l], sgu_w[l], sgu_b[l],
                   q_norm_g[l], k_norm_g[l], out_norm_a_g[l], out_norm_b_g[l], w_out[l],
                   norm2_g[l], w_gate[l], w_up[l], w_down[l])
    return x
```

```python
import jax
import jax.numpy as jnp
from jax import lax
from jax.experimental import pallas as pl
from jax.experimental.pallas import tpu as pltpu

D_MODEL = 1024
HEAD_DIM = 64
N_HEADS = 8
WIDTH = N_HEADS * HEAD_DIM
CHUNK = 128
BLK = 256
TOPK = 3
D_FF = 2816
EPS = 1e-6
Q_SCALE = HEAD_DIM ** -0.5 * 1.4426950408889634

TM1 = 1024
SUB1 = 256
TM3 = 512
HEADS_PER_STEP = 2
PAIR = HEADS_PER_STEP * HEAD_DIM
SCORES_AHEAD = 2

COL_LN_G, COL_LN_B, COL_A_G, COL_B_G, COL_Q_G, COL_K_G = range(6)
N_COLS = 6

VMEM_LIMIT_BYTES = 56 * 1024 * 1024

_NT = (((1,), (1,)), ((), ()))
_F32 = jnp.float32
_BF16 = jnp.bfloat16


def _head_rms(t, gain):
    ms = jnp.mean(t * t, axis=0, keepdims=True)
    return t * lax.rsqrt(ms + EPS) * gain


def _inproj_kernel(x_ref, g1_ref, wu_ref, wv_ref, wq_ref, wk_ref, wvb_ref, cols_ref, sw_ref, sb_ref,
                   ya_ref, qT_ref, k_ref, vT_ref, km_ref):
    si = pl.program_id(1)
    nsub = TM1 // SUB1
    row = lax.broadcasted_iota(jnp.int32, (CHUNK, CHUNK), 0)
    col = lax.broadcasted_iota(jnp.int32, (CHUNK, CHUNK), 1)
    causal = col <= row
    w_mix = [jnp.where(causal, sw_ref[hd], 0.0).astype(_BF16) for hd in range(N_HEADS)]

    def project(t):
        x = x_ref[0, t * SUB1:(t + 1) * SUB1, :]
        ms = jnp.mean(x * x, axis=-1, keepdims=True)
        h = (x * lax.rsqrt(ms + EPS) * g1_ref[...]).astype(_BF16)
        return [lax.dot_general(w_ref[...], h, _NT, preferred_element_type=_F32)
                for w_ref in (wu_ref, wv_ref, wq_ref, wk_ref, wvb_ref)]

    pending = []
    for t in range(nsub):
        pending.append((t, project(t)))
        if len(pending) > 1:
            _inproj_finish(*pending.pop(0), si * nsub, w_mix, cols_ref, sb_ref,
                           ya_ref, qT_ref, k_ref, vT_ref, km_ref)
    _inproj_finish(*pending.pop(0), si * nsub, w_mix, cols_ref, sb_ref,
                   ya_ref, qT_ref, k_ref, vT_ref, km_ref)


def _inproj_finish(t, projs, blk0, w_mix, cols_ref, sb_ref, ya_ref, qT_ref, k_ref, vT_ref, km_ref):
    pu, pv, qt, kt, vbt = projs
    nchunk = SUB1 // CHUNK
    rows = slice(t * SUB1, (t + 1) * SUB1)

    u = jax.nn.gelu(pu)
    v = jax.nn.gelu(pv)
    ya = []
    for hd in range(N_HEADS):
        sl = slice(hd * HEAD_DIM, (hd + 1) * HEAD_DIM)
        vh = v[sl, :]
        mu = jnp.mean(vh, axis=0, keepdims=True)
        vc = vh - mu
        var = jnp.mean(vc * vc, axis=0, keepdims=True)
        vn = (vc * lax.rsqrt(var + EPS) * cols_ref[sl, COL_LN_G:COL_LN_G + 1]
              + cols_ref[sl, COL_LN_B:COL_LN_B + 1])
        lhs = jnp.concatenate(
            [vn[:, c * CHUNK:(c + 1) * CHUNK] for c in range(nchunk)], axis=0).astype(_BF16)
        mixed = lax.dot_general(lhs, w_mix[hd], _NT, preferred_element_type=_F32)
        mixed = mixed + sb_ref[hd:hd + 1, :]
        mixed = jnp.concatenate(
            [mixed[c * HEAD_DIM:(c + 1) * HEAD_DIM, :] for c in range(nchunk)], axis=1)
        ya.append(_head_rms(u[sl, :] * mixed, cols_ref[sl, COL_A_G:COL_A_G + 1]))
    ya_ref[0, rows, :] = jnp.concatenate(ya, axis=0).T.astype(_BF16)

    qn = jnp.concatenate(
        [_head_rms(qt[hd * HEAD_DIM:(hd + 1) * HEAD_DIM, :], cols_ref[0:HEAD_DIM, COL_Q_G:COL_Q_G + 1])
         for hd in range(N_HEADS)], axis=0) * Q_SCALE
    kn = jnp.concatenate(
        [_head_rms(kt[hd * HEAD_DIM:(hd + 1) * HEAD_DIM, :], cols_ref[0:HEAD_DIM, COL_K_G:COL_K_G + 1])
         for hd in range(N_HEADS)], axis=0)
    k_tok = kn.T
    qT_ref[0, :, rows] = qn.astype(_BF16)
    vT_ref[0, :, rows] = vbt.astype(_BF16)
    k_ref[0, rows, :] = k_tok.astype(_BF16)
    for b in range(SUB1 // BLK):
        kb = k_tok[b * BLK:(b + 1) * BLK, :]
        km_ref[0, pl.ds((blk0 + t) * (SUB1 // BLK) + b, 1), :] = jnp.mean(kb, axis=0, keepdims=True)


def _attn_kernel(qT_ref, k_ref, vT_ref, km_ref, cols_ref, yb_ref):
    nb = k_ref.shape[1] // BLK
    frow = lax.broadcasted_iota(jnp.int32, (PAIR, BLK), 0)
    key_pos = lax.broadcasted_iota(jnp.int32, (BLK, BLK), 0)
    qry_pos = lax.broadcasted_iota(jnp.int32, (BLK, BLK), 1)
    causal_bias = jnp.where(key_pos <= qry_pos, 0.0, -jnp.inf)

    def scores(jj, hh):
        nkeys = (jj + 1) * BLK
        qT = qT_ref[0, :, jj * BLK:(jj + 1) * BLK]
        in_head = (frow >= hh * HEAD_DIM) & (frow < (hh + 1) * HEAD_DIM)
        qm = jnp.where(in_head, qT, jnp.zeros_like(qT))
        bias = None
        if jj > TOPK:
            km = km_ref[0, 0:jj, :]
            km_hi = km.astype(_BF16)
            km_lo = (km - km_hi.astype(_F32)).astype(_BF16)
            blk_id = lax.broadcasted_iota(jnp.int32, (jj, BLK), 0)
            gate = (jnp.dot(km_hi, qm, preferred_element_type=_F32)
                    + jnp.dot(km_lo, qm, preferred_element_type=_F32))
            rank = jnp.zeros((jj, BLK), _F32)
            for m in range(jj):
                gm = gate[m:m + 1, :]
                ahead = (gm > gate) | ((gm == gate) & (blk_id > m))
                rank = rank + jnp.where(ahead, 1.0, 0.0)
            bias = jnp.where(rank < TOPK, 0.0, -jnp.inf)
        s = jnp.dot(k_ref[0, 0:nkeys, :], qm, preferred_element_type=_F32)
        parts = []
        for n in range(jj):
            sn = s[n * BLK:(n + 1) * BLK, :]
            parts.append(sn if bias is None else sn + bias[n:n + 1, :])
        parts.append(s[jj * BLK:nkeys, :] + causal_bias)
        s = jnp.concatenate(parts, axis=0) if jj > 0 else parts[0]
        return s, jnp.max(s, axis=0, keepdims=True)

    tiles = {}

    def finish(jj, hh, s, m):
        p = jnp.exp2(s - m)
        l = jnp.sum(p, axis=0, keepdims=True)
        vrows = slice(hh * HEAD_DIM, (hh + 1) * HEAD_DIM)
        o = jnp.dot(vT_ref[0, vrows, 0:(jj + 1) * BLK], p.astype(_BF16), preferred_element_type=_F32)
        tiles.setdefault(jj, []).append(
            _head_rms(o / l, cols_ref[vrows, COL_B_G:COL_B_G + 1]))
        if len(tiles[jj]) == HEADS_PER_STEP:
            y = jnp.concatenate(tiles.pop(jj), axis=0)
            yb_ref[0, jj * BLK:(jj + 1) * BLK, :] = y.T.astype(_BF16)

    pending = []
    for jj in range(nb):
        for hh in range(HEADS_PER_STEP):
            pending.append((jj, hh, *scores(jj, hh)))
            if len(pending) > SCORES_AHEAD:
                finish(*pending.pop(0))
    for job in pending:
        finish(*job)


def _ffn_kernel(x_ref, ya_ref, yb_ref, wo_ref, g2_ref, wg_ref, wu_ref, wd_ref, o_ref):
    y = jnp.concatenate([ya_ref[...], yb_ref[...]], axis=1)
    x1 = x_ref[...] + jnp.dot(y, wo_ref[...], preferred_element_type=_F32)
    ms = jnp.mean(x1 * x1, axis=-1, keepdims=True)
    h = (x1 * lax.rsqrt(ms + EPS) * g2_ref[...]).astype(_BF16)
    g = jnp.dot(h, wg_ref[...], preferred_element_type=_F32)
    u = jnp.dot(h, wu_ref[...], preferred_element_type=_F32)
    a = (jax.nn.silu(g) * u).astype(_BF16)
    o_ref[...] = x1 + jnp.dot(a, wd_ref[...], preferred_element_type=_F32)


def _full(shape):
    return pl.BlockSpec(shape, lambda *_: (0,) * len(shape), pipeline_mode=pl.Buffered(1))


def _layer(x, norm1_g, w_in, sgu_ln_g, sgu_ln_b, sgu_w, sgu_b, q_norm_g, k_norm_g,
           out_norm_a_g, out_norm_b_g, w_out, norm2_g, w_gate, w_up, w_down):
    B, S, D = x.shape
    nb = S // BLK
    T = B * S
    assert D == D_MODEL and S % TM1 == 0 and T % TM3 == 0

    w_in_t = w_in.astype(_BF16).T
    w_parts = [w_in_t[i * WIDTH:(i + 1) * WIDTH] for i in range(5)]
    cols = [None] * N_COLS
    cols[COL_LN_G], cols[COL_LN_B] = sgu_ln_g, sgu_ln_b
    cols[COL_A_G], cols[COL_B_G] = out_norm_a_g, out_norm_b_g
    cols[COL_Q_G], cols[COL_K_G] = jnp.tile(q_norm_g, N_HEADS), jnp.tile(k_norm_g, N_HEADS)
    cols = jnp.stack(cols, axis=1).astype(_F32)

    ya, qT, k, vT, km = pl.pallas_call(
        _inproj_kernel,
        grid=(B, S // TM1),
        in_specs=[
            pl.BlockSpec((1, TM1, D), lambda b, s: (b, s, 0)),
            _full((1, D)),
            *[_full((WIDTH, D))] * 5,
            _full((WIDTH, N_COLS)),
            _full((N_HEADS, CHUNK, CHUNK)), _full((N_HEADS, CHUNK)),
        ],
        out_specs=[
            pl.BlockSpec((1, TM1, WIDTH), lambda b, s: (b, s, 0)),
            pl.BlockSpec((1, WIDTH, TM1), lambda b, s: (b, 0, s)),
            pl.BlockSpec((1, TM1, WIDTH), lambda b, s: (b, s, 0)),
            pl.BlockSpec((1, WIDTH, TM1), lambda b, s: (b, 0, s)),
            pl.BlockSpec((1, nb, WIDTH), lambda b, s: (b, 0, 0)),
        ],
        out_shape=[
            jax.ShapeDtypeStruct((B, S, WIDTH), _BF16),
            jax.ShapeDtypeStruct((B, WIDTH, S), _BF16),
            jax.ShapeDtypeStruct((B, S, WIDTH), _BF16),
            jax.ShapeDtypeStruct((B, WIDTH, S), _BF16),
            jax.ShapeDtypeStruct((B, nb, WIDTH), _F32),
        ],
        compiler_params=pltpu.CompilerParams(
            dimension_semantics=("parallel", "arbitrary"),
            vmem_limit_bytes=VMEM_LIMIT_BYTES),
        name="inproj_sgu",
    )(x, norm1_g.reshape(1, D), *w_parts, cols, sgu_w, sgu_b)

    n_pair = WIDTH // PAIR
    yb = pl.pallas_call(
        _attn_kernel,
        grid=(B, n_pair),
        in_specs=[
            pl.BlockSpec((1, PAIR, S), lambda b, p: (b, p, 0)),
            pl.BlockSpec((1, S, PAIR), lambda b, p: (b, 0, p)),
            pl.BlockSpec((1, PAIR, S), lambda b, p: (b, p, 0)),
            pl.BlockSpec((1, nb, PAIR), lambda b, p: (b, 0, p)),
            pl.BlockSpec((PAIR, N_COLS), lambda b, p: (p, 0)),
        ],
        out_specs=pl.BlockSpec((1, S, PAIR), lambda b, p: (b, 0, p)),
        out_shape=jax.ShapeDtypeStruct((B, S, WIDTH), _BF16),
        compiler_params=pltpu.CompilerParams(
            dimension_semantics=("parallel", "parallel"),
            vmem_limit_bytes=VMEM_LIMIT_BYTES),
        name="moba_attn",
    )(qT, k, vT, km, cols)

    out = pl.pallas_call(
        _ffn_kernel,
        grid=(T // TM3,),
        in_specs=[
            pl.BlockSpec((TM3, D), lambda i: (i, 0)),
            pl.BlockSpec((TM3, WIDTH), lambda i: (i, 0)),
            pl.BlockSpec((TM3, WIDTH), lambda i: (i, 0)),
            _full((2 * WIDTH, D)),
            _full((1, D)),
            _full((D, D_FF)), _full((D, D_FF)), _full((D_FF, D)),
        ],
        out_specs=pl.BlockSpec((TM3, D), lambda i: (i, 0)),
        out_shape=jax.ShapeDtypeStruct((T, D), _F32),
        compiler_params=pltpu.CompilerParams(
            dimension_semantics=("parallel",),
            vmem_limit_bytes=VMEM_LIMIT_BYTES),
        name="outproj_ffn",
    )(x.reshape(T, D), ya.reshape(T, WIDTH), yb.reshape(T, WIDTH), w_out.astype(_BF16),
      norm2_g.reshape(1, D), w_gate.astype(_BF16), w_up.astype(_BF16), w_down.astype(_BF16))
    return out.reshape(B, S, D)


def kernel(x, norm1_g, w_in, sgu_ln_g, sgu_ln_b, sgu_w, sgu_b, q_norm_g, k_norm_g,
           out_norm_a_g, out_norm_b_g, w_out, norm2_g, w_gate, w_up, w_down):
    depth = norm1_g.shape[0]
    for l in range(depth):
        x = _layer(x, norm1_g[l], w_in[l], sgu_ln_g[l], sgu_ln_b[l], sgu_w[l], sgu_b[l],
                   q_norm_g[l], k_norm_g[l], out_norm_a_g[l], out_norm_b_g[l], w_out[l],
                   norm2_g[l], w_gate[l], w_up[l], w_down[l])
    return x
```

```python
import jax
import jax.numpy as jnp
from jax import lax
from jax.experimental import pallas as pl
from jax.experimental.pallas import tpu as pltpu

D_MODEL = 1024
HEAD_DIM = 64
N_HEADS = 8
WIDTH = N_HEADS * HEAD_DIM
CHUNK = 128
BLK = 256
TOPK = 3
D_FF = 2816
EPS = 1e-6
Q_SCALE = HEAD_DIM ** -0.5 * 1.4426950408889634

TM1 = 1024
SUB1 = 256
TM3 = 512
HEADS_PER_STEP = 2
PAIR = HEADS_PER_STEP * HEAD_DIM
SCORES_AHEAD = 2

COL_LN_G, COL_LN_B, COL_A_G, COL_B_G, COL_Q_G, COL_K_G = range(6)
N_COLS = 6

VMEM_LIMIT_BYTES = 56 * 1024 * 1024

_NT = (((1,), (1,)), ((), ()))
_F32 = jnp.float32
_BF16 = jnp.bfloat16


def _head_rms(t, gain):
    ms = jnp.mean(t * t, axis=0, keepdims=True)
    return t * lax.rsqrt(ms + EPS) * gain


def _inproj_kernel(x_ref, g1_ref, wu_ref, wv_ref, wq_ref, wk_ref, wvb_ref, cols_ref, sw_ref, sb_ref,
                   wo32_ref, wg32_ref, wup32_ref, wd32_ref,
                   ya_ref, qT_ref, k_ref, vT_ref, km_ref, wo_ref, wg_ref, wup_ref, wd_ref):
    for src, dst in ((wo32_ref, wo_ref), (wg32_ref, wg_ref), (wup32_ref, wup_ref), (wd32_ref, wd_ref)):
        dst[...] = src[...].astype(_BF16)

    si = pl.program_id(1)
    nsub = TM1 // SUB1
    row = lax.broadcasted_iota(jnp.int32, (CHUNK, CHUNK), 0)
    col = lax.broadcasted_iota(jnp.int32, (CHUNK, CHUNK), 1)
    causal = col <= row
    w_mix = [jnp.where(causal, sw_ref[hd], 0.0).astype(_BF16) for hd in range(N_HEADS)]

    def project(t):
        x = x_ref[0, t * SUB1:(t + 1) * SUB1, :]
        ms = jnp.mean(x * x, axis=-1, keepdims=True)
        h = (x * lax.rsqrt(ms + EPS) * g1_ref[...]).astype(_BF16)
        return [lax.dot_general(w_ref[...], h, _NT, preferred_element_type=_F32)
                for w_ref in (wu_ref, wv_ref, wq_ref, wk_ref, wvb_ref)]

    pending = []
    for t in range(nsub):
        pending.append((t, project(t)))
        if len(pending) > 1:
            _inproj_finish(*pending.pop(0), si * nsub, w_mix, cols_ref, sb_ref,
                           ya_ref, qT_ref, k_ref, vT_ref, km_ref)
    _inproj_finish(*pending.pop(0), si * nsub, w_mix, cols_ref, sb_ref,
                   ya_ref, qT_ref, k_ref, vT_ref, km_ref)


def _inproj_finish(t, projs, blk0, w_mix, cols_ref, sb_ref, ya_ref, qT_ref, k_ref, vT_ref, km_ref):
    pu, pv, qt, kt, vbt = projs
    nchunk = SUB1 // CHUNK
    rows = slice(t * SUB1, (t + 1) * SUB1)

    u = jax.nn.gelu(pu)
    v = jax.nn.gelu(pv)
    ya = []
    for hd in range(N_HEADS):
        sl = slice(hd * HEAD_DIM, (hd + 1) * HEAD_DIM)
        vh = v[sl, :]
        mu = jnp.mean(vh, axis=0, keepdims=True)
        vc = vh - mu
        var = jnp.mean(vc * vc, axis=0, keepdims=True)
        vn = (vc * lax.rsqrt(var + EPS) * cols_ref[sl, COL_LN_G:COL_LN_G + 1]
              + cols_ref[sl, COL_LN_B:COL_LN_B + 1])
        lhs = jnp.concatenate(
            [vn[:, c * CHUNK:(c + 1) * CHUNK] for c in range(nchunk)], axis=0).astype(_BF16)
        mixed = lax.dot_general(lhs, w_mix[hd], _NT, preferred_element_type=_F32)
        mixed = mixed + sb_ref[hd:hd + 1, :]
        mixed = jnp.concatenate(
            [mixed[c * HEAD_DIM:(c + 1) * HEAD_DIM, :] for c in range(nchunk)], axis=1)
        ya.append(_head_rms(u[sl, :] * mixed, cols_ref[sl, COL_A_G:COL_A_G + 1]))
    ya_ref[0, rows, :] = jnp.concatenate(ya, axis=0).T.astype(_BF16)

    qn = jnp.concatenate(
        [_head_rms(qt[hd * HEAD_DIM:(hd + 1) * HEAD_DIM, :], cols_ref[0:HEAD_DIM, COL_Q_G:COL_Q_G + 1])
         for hd in range(N_HEADS)], axis=0) * Q_SCALE
    kn = jnp.concatenate(
        [_head_rms(kt[hd * HEAD_DIM:(hd + 1) * HEAD_DIM, :], cols_ref[0:HEAD_DIM, COL_K_G:COL_K_G + 1])
         for hd in range(N_HEADS)], axis=0)
    k_tok = kn.T
    qT_ref[0, :, rows] = qn.astype(_BF16)
    vT_ref[0, :, rows] = vbt.astype(_BF16)
    k_ref[0, rows, :] = k_tok.astype(_BF16)
    for b in range(SUB1 // BLK):
        kb = k_tok[b * BLK:(b + 1) * BLK, :]
        km_ref[0, pl.ds((blk0 + t) * (SUB1 // BLK) + b, 1), :] = jnp.mean(kb, axis=0, keepdims=True)


def _attn_kernel(qT_ref, k_ref, vT_ref, km_ref, cols_ref, yb_ref):
    nb = k_ref.shape[1] // BLK
    frow = lax.broadcasted_iota(jnp.int32, (PAIR, BLK), 0)
    key_pos = lax.broadcasted_iota(jnp.int32, (BLK, BLK), 0)
    qry_pos = lax.broadcasted_iota(jnp.int32, (BLK, BLK), 1)
    causal_bias = jnp.where(key_pos <= qry_pos, 0.0, -jnp.inf)

    def scores(jj, hh):
        nkeys = (jj + 1) * BLK
        qT = qT_ref[0, :, jj * BLK:(jj + 1) * BLK]
        in_head = (frow >= hh * HEAD_DIM) & (frow < (hh + 1) * HEAD_DIM)
        qm = jnp.where(in_head, qT, jnp.zeros_like(qT))
        bias = None
        if jj > TOPK:
            km = km_ref[0, 0:jj, :]
            km_hi = km.astype(_BF16)
            km_lo = (km - km_hi.astype(_F32)).astype(_BF16)
            blk_id = lax.broadcasted_iota(jnp.int32, (jj, BLK), 0)
            gate = (jnp.dot(km_hi, qm, preferred_element_type=_F32)
                    + jnp.dot(km_lo, qm, preferred_element_type=_F32))
            rank = jnp.zeros((jj, BLK), _F32)
            for m in range(jj):
                gm = gate[m:m + 1, :]
                ahead = (gm > gate) | ((gm == gate) & (blk_id > m))
                rank = rank + jnp.where(ahead, 1.0, 0.0)
            bias = jnp.where(rank < TOPK, 0.0, -jnp.inf)
        s = jnp.dot(k_ref[0, 0:nkeys, :], qm, preferred_element_type=_F32)
        blocks, m = [], None
        for n in range(jj + 1):
            sn = s[n * BLK:(n + 1) * BLK, :]
            if n == jj:
                sn = sn + causal_bias
            elif bias is not None:
                sn = sn + bias[n:n + 1, :]
            blocks.append(sn)
            mn = jnp.max(sn, axis=0, keepdims=True)
            m = mn if m is None else jnp.maximum(m, mn)
        return blocks, m

    tiles = {}

    def finish(jj, hh, blocks, m):
        probs, l = [], None
        for sn in blocks:
            p = jnp.exp2(sn - m)
            ln = jnp.sum(p, axis=0, keepdims=True)
            l = ln if l is None else l + ln
            probs.append(p.astype(_BF16))
        p = jnp.concatenate(probs, axis=0) if jj > 0 else probs[0]
        vrows = slice(hh * HEAD_DIM, (hh + 1) * HEAD_DIM)
        o = jnp.dot(vT_ref[0, vrows, 0:(jj + 1) * BLK], p, preferred_element_type=_F32)
        tiles.setdefault(jj, []).append(
            _head_rms(o / l, cols_ref[vrows, COL_B_G:COL_B_G + 1]))
        if len(tiles[jj]) == HEADS_PER_STEP:
            y = jnp.concatenate(tiles.pop(jj), axis=0)
            yb_ref[0, jj * BLK:(jj + 1) * BLK, :] = y.T.astype(_BF16)

    pending = []
    for jj in range(nb):
        for hh in range(HEADS_PER_STEP):
            pending.append((jj, hh, *scores(jj, hh)))
            if len(pending) > SCORES_AHEAD:
                finish(*pending.pop(0))
    for job in pending:
        finish(*job)


def _ffn_kernel(x_ref, ya_ref, yb_ref, wo_ref, g2_ref, wg_ref, wu_ref, wd_ref, o_ref):
    y = jnp.concatenate([ya_ref[...], yb_ref[...]], axis=1)
    x1 = x_ref[...] + jnp.dot(y, wo_ref[...], preferred_element_type=_F32)
    ms = jnp.mean(x1 * x1, axis=-1, keepdims=True)
    h = (x1 * lax.rsqrt(ms + EPS) * g2_ref[...]).astype(_BF16)
    g = jnp.dot(h, wg_ref[...], preferred_element_type=_F32)
    u = jnp.dot(h, wu_ref[...], preferred_element_type=_F32)
    a = (jax.nn.silu(g) * u).astype(_BF16)
    o_ref[...] = x1 + jnp.dot(a, wd_ref[...], preferred_element_type=_F32)


def _full(shape):
    return pl.BlockSpec(shape, lambda *_: (0,) * len(shape), pipeline_mode=pl.Buffered(1))


def _layer(x, norm1_g, w_in, sgu_ln_g, sgu_ln_b, sgu_w, sgu_b, q_norm_g, k_norm_g,
           out_norm_a_g, out_norm_b_g, w_out, norm2_g, w_gate, w_up, w_down):
    B, S, D = x.shape
    nb = S // BLK
    T = B * S
    assert D == D_MODEL and S % TM1 == 0 and T % TM3 == 0

    w_in_t = w_in.astype(_BF16).T
    w_parts = [w_in_t[i * WIDTH:(i + 1) * WIDTH] for i in range(5)]
    cols = [None] * N_COLS
    cols[COL_LN_G], cols[COL_LN_B] = sgu_ln_g, sgu_ln_b
    cols[COL_A_G], cols[COL_B_G] = out_norm_a_g, out_norm_b_g
    cols[COL_Q_G], cols[COL_K_G] = jnp.tile(q_norm_g, N_HEADS), jnp.tile(k_norm_g, N_HEADS)
    cols = jnp.stack(cols, axis=1).astype(_F32)

    steps1 = B * (S // TM1)
    ffn_w = (w_out, w_gate, w_up, w_down)
    assert all(w.shape[0] % (16 * steps1) == 0 for w in ffn_w)
    slab_specs = [pl.BlockSpec((w.shape[0] // steps1, w.shape[1]), lambda b, s: (b * (S // TM1) + s, 0))
                  for w in ffn_w]

    ya, qT, k, vT, km, wo_b, wg_b, wup_b, wd_b = pl.pallas_call(
        _inproj_kernel,
        grid=(B, S // TM1),
        in_specs=[
            pl.BlockSpec((1, TM1, D), lambda b, s: (b, s, 0)),
            _full((1, D)),
            *[_full((WIDTH, D))] * 5,
            _full((WIDTH, N_COLS)),
            _full((N_HEADS, CHUNK, CHUNK)), _full((N_HEADS, CHUNK)),
            *slab_specs,
        ],
        out_specs=[
            pl.BlockSpec((1, TM1, WIDTH), lambda b, s: (b, s, 0)),
            pl.BlockSpec((1, WIDTH, TM1), lambda b, s: (b, 0, s)),
            pl.BlockSpec((1, TM1, WIDTH), lambda b, s: (b, s, 0)),
            pl.BlockSpec((1, WIDTH, TM1), lambda b, s: (b, 0, s)),
            pl.BlockSpec((1, nb, WIDTH), lambda b, s: (b, 0, 0)),
            *slab_specs,
        ],
        out_shape=[
            jax.ShapeDtypeStruct((B, S, WIDTH), _BF16),
            jax.ShapeDtypeStruct((B, WIDTH, S), _BF16),
            jax.ShapeDtypeStruct((B, S, WIDTH), _BF16),
            jax.ShapeDtypeStruct((B, WIDTH, S), _BF16),
            jax.ShapeDtypeStruct((B, nb, WIDTH), _F32),
            *[jax.ShapeDtypeStruct(w.shape, _BF16) for w in ffn_w],
        ],
        compiler_params=pltpu.CompilerParams(
            dimension_semantics=("parallel", "arbitrary"),
            vmem_limit_bytes=VMEM_LIMIT_BYTES),
        name="inproj_sgu",
    )(x, norm1_g.reshape(1, D), *w_parts, cols, sgu_w, sgu_b, *ffn_w)

    n_pair = WIDTH // PAIR
    yb = pl.pallas_call(
        _attn_kernel,
        grid=(B, n_pair),
        in_specs=[
            pl.BlockSpec((1, PAIR, S), lambda b, p: (b, p, 0)),
            pl.BlockSpec((1, S, PAIR), lambda b, p: (b, 0, p)),
            pl.BlockSpec((1, PAIR, S), lambda b, p: (b, p, 0)),
            pl.BlockSpec((1, nb, PAIR), lambda b, p: (b, 0, p)),
            pl.BlockSpec((PAIR, N_COLS), lambda b, p: (p, 0)),
        ],
        out_specs=pl.BlockSpec((1, S, PAIR), lambda b, p: (b, 0, p)),
        out_shape=jax.ShapeDtypeStruct((B, S, WIDTH), _BF16),
        compiler_params=pltpu.CompilerParams(
            dimension_semantics=("parallel", "parallel"),
            vmem_limit_bytes=VMEM_LIMIT_BYTES),
        name="moba_attn",
    )(qT, k, vT, km, cols)

    out = pl.pallas_call(
        _ffn_kernel,
        grid=(T // TM3,),
        in_specs=[
            pl.BlockSpec((TM3, D), lambda i: (i, 0)),
            pl.BlockSpec((TM3, WIDTH), lambda i: (i, 0)),
            pl.BlockSpec((TM3, WIDTH), lambda i: (i, 0)),
            _full((2 * WIDTH, D)),
            _full((1, D)),
            _full((D, D_FF)), _full((D, D_FF)), _full((D_FF, D)),
        ],
        out_specs=pl.BlockSpec((TM3, D), lambda i: (i, 0)),
        out_shape=jax.ShapeDtypeStruct((T, D), _F32),
        compiler_params=pltpu.CompilerParams(
            dimension_semantics=("parallel",),
            vmem_limit_bytes=VMEM_LIMIT_BYTES),
        name="outproj_ffn",
    )(x.reshape(T, D), ya.reshape(T, WIDTH), yb.reshape(T, WIDTH), wo_b, norm2_g.reshape(1, D),
      wg_b, wup_b, wd_b)
    return out.reshape(B, S, D)


def kernel(x, norm1_g, w_in, sgu_ln_g, sgu_ln_b, sgu_w, sgu_b, q_norm_g, k_norm_g,
           out_norm_a_g, out_norm_b_g, w_out, norm2_g, w_gate, w_up, w_down):
    depth = norm1_g.shape[0]
    for l in range(depth):
        x = _layer(x, norm1_g[l], w_in[l], sgu_ln_g[l], sgu_ln_b[l], sgu_w[l], sgu_b[l],
                   q_norm_g[l], k_norm_g[l], out_norm_a_g[l], out_norm_b_g[l], w_out[l],
                   norm2_g[l], w_gate[l], w_up[l], w_down[l])
    return x
```

```python
import jax
import jax.numpy as jnp
from jax import lax
from jax.experimental import pallas as pl
from jax.experimental.pallas import tpu as pltpu

D_MODEL = 1024
HEAD_DIM = 64
N_HEADS = 8
WIDTH = N_HEADS * HEAD_DIM
CHUNK = 128
BLK = 256
TOPK = 3
D_FF = 2816
EPS = 1e-6
Q_SCALE = HEAD_DIM ** -0.5 * 1.4426950408889634

TM1 = 1024
SUB1 = 256
TM3 = 512
SUB3 = 256
HEADS_PER_STEP = 2
PAIR = HEADS_PER_STEP * HEAD_DIM
SCORES_AHEAD = 2

COL_LN_G, COL_LN_B, COL_A_G, COL_B_G, COL_Q_G, COL_K_G = range(6)
N_COLS = 6

VMEM_LIMIT_BYTES = 56 * 1024 * 1024

_NT = (((1,), (1,)), ((), ()))
_F32 = jnp.float32
_BF16 = jnp.bfloat16


def _head_rms(t, gain):
    ms = jnp.mean(t * t, axis=0, keepdims=True)
    return t * lax.rsqrt(ms + EPS) * gain


def _inproj_kernel(x_ref, g1_ref, wu_ref, wv_ref, wq_ref, wk_ref, wvb_ref, cols_ref, sw_ref, sb_ref,
                   wo32_ref, wg32_ref, wup32_ref, wd32_ref,
                   ya_ref, qT_ref, k_ref, vT_ref, km_ref, wo_ref, wg_ref, wup_ref, wd_ref):
    for src, dst in ((wo32_ref, wo_ref), (wg32_ref, wg_ref), (wup32_ref, wup_ref), (wd32_ref, wd_ref)):
        dst[...] = src[...].astype(_BF16)

    si = pl.program_id(1)
    nsub = TM1 // SUB1
    row = lax.broadcasted_iota(jnp.int32, (CHUNK, CHUNK), 0)
    col = lax.broadcasted_iota(jnp.int32, (CHUNK, CHUNK), 1)
    causal = col <= row
    w_mix = [jnp.where(causal, sw_ref[hd], 0.0).astype(_BF16) for hd in range(N_HEADS)]

    def project(t):
        x = x_ref[0, t * SUB1:(t + 1) * SUB1, :]
        ms = jnp.mean(x * x, axis=-1, keepdims=True)
        h = (x * lax.rsqrt(ms + EPS) * g1_ref[...]).astype(_BF16)
        return [lax.dot_general(w_ref[...], h, _NT, preferred_element_type=_F32)
                for w_ref in (wu_ref, wv_ref, wq_ref, wk_ref, wvb_ref)]

    pending = []
    for t in range(nsub):
        pending.append((t, project(t)))
        if len(pending) > 1:
            _inproj_finish(*pending.pop(0), si * nsub, w_mix, cols_ref, sb_ref,
                           ya_ref, qT_ref, k_ref, vT_ref, km_ref)
    _inproj_finish(*pending.pop(0), si * nsub, w_mix, cols_ref, sb_ref,
                   ya_ref, qT_ref, k_ref, vT_ref, km_ref)


def _inproj_finish(t, projs, blk0, w_mix, cols_ref, sb_ref, ya_ref, qT_ref, k_ref, vT_ref, km_ref):
    pu, pv, qt, kt, vbt = projs
    nchunk = SUB1 // CHUNK
    rows = slice(t * SUB1, (t + 1) * SUB1)

    u = jax.nn.gelu(pu)
    v = jax.nn.gelu(pv)
    ya = []
    for hd in range(N_HEADS):
        sl = slice(hd * HEAD_DIM, (hd + 1) * HEAD_DIM)
        vh = v[sl, :]
        mu = jnp.mean(vh, axis=0, keepdims=True)
        vc = vh - mu
        var = jnp.mean(vc * vc, axis=0, keepdims=True)
        vn = (vc * lax.rsqrt(var + EPS) * cols_ref[sl, COL_LN_G:COL_LN_G + 1]
              + cols_ref[sl, COL_LN_B:COL_LN_B + 1])
        lhs = jnp.concatenate(
            [vn[:, c * CHUNK:(c + 1) * CHUNK] for c in range(nchunk)], axis=0).astype(_BF16)
        mixed = lax.dot_general(lhs, w_mix[hd], _NT, preferred_element_type=_F32)
        mixed = mixed + sb_ref[hd:hd + 1, :]
        mixed = jnp.concatenate(
            [mixed[c * HEAD_DIM:(c + 1) * HEAD_DIM, :] for c in range(nchunk)], axis=1)
        ya.append(_head_rms(u[sl, :] * mixed, cols_ref[sl, COL_A_G:COL_A_G + 1]))
    ya_ref[0, rows, :] = jnp.concatenate(ya, axis=0).T.astype(_BF16)

    qn = jnp.concatenate(
        [_head_rms(qt[hd * HEAD_DIM:(hd + 1) * HEAD_DIM, :], cols_ref[0:HEAD_DIM, COL_Q_G:COL_Q_G + 1])
         for hd in range(N_HEADS)], axis=0) * Q_SCALE
    kn = jnp.concatenate(
        [_head_rms(kt[hd * HEAD_DIM:(hd + 1) * HEAD_DIM, :], cols_ref[0:HEAD_DIM, COL_K_G:COL_K_G + 1])
         for hd in range(N_HEADS)], axis=0)
    k_tok = kn.T
    qT_ref[0, :, rows] = qn.astype(_BF16)
    vT_ref[0, :, rows] = vbt.astype(_BF16)
    k_ref[0, rows, :] = k_tok.astype(_BF16)
    for b in range(SUB1 // BLK):
        kb = k_tok[b * BLK:(b + 1) * BLK, :]
        km_ref[0, pl.ds((blk0 + t) * (SUB1 // BLK) + b, 1), :] = jnp.mean(kb, axis=0, keepdims=True)


def _attn_jobs(qT_ref, k_ref, vT_ref, km_ref, cols_ref, store_tile):
    nb = k_ref.shape[1] // BLK
    frow = lax.broadcasted_iota(jnp.int32, (PAIR, BLK), 0)
    key_pos = lax.broadcasted_iota(jnp.int32, (BLK, BLK), 0)
    qry_pos = lax.broadcasted_iota(jnp.int32, (BLK, BLK), 1)
    causal_bias = jnp.where(key_pos <= qry_pos, 0.0, -jnp.inf)

    def scores(jj, hh):
        nkeys = (jj + 1) * BLK
        qT = qT_ref[0, :, jj * BLK:(jj + 1) * BLK]
        in_head = (frow >= hh * HEAD_DIM) & (frow < (hh + 1) * HEAD_DIM)
        qm = jnp.where(in_head, qT, jnp.zeros_like(qT))
        bias = None
        if jj > TOPK:
            km = km_ref[0, 0:jj, :]
            km_hi = km.astype(_BF16)
            km_lo = (km - km_hi.astype(_F32)).astype(_BF16)
            blk_id = lax.broadcasted_iota(jnp.int32, (jj, BLK), 0)
            gate = (jnp.dot(km_hi, qm, preferred_element_type=_F32)
                    + jnp.dot(km_lo, qm, preferred_element_type=_F32))
            rank = jnp.zeros((jj, BLK), _F32)
            for m in range(jj):
                gm = gate[m:m + 1, :]
                ahead = (gm > gate) | ((gm == gate) & (blk_id > m))
                rank = rank + jnp.where(ahead, 1.0, 0.0)
            bias = jnp.where(rank < TOPK, 0.0, -jnp.inf)
        s = jnp.dot(k_ref[0, 0:nkeys, :], qm, preferred_element_type=_F32)
        blocks, m = [], None
        for n in range(jj + 1):
            sn = s[n * BLK:(n + 1) * BLK, :]
            if n == jj:
                sn = sn + causal_bias
            elif bias is not None:
                sn = sn + bias[n:n + 1, :]
            blocks.append(sn)
            mn = jnp.max(sn, axis=0, keepdims=True)
            m = mn if m is None else jnp.maximum(m, mn)
        return blocks, m

    tiles = {}

    def finish(jj, hh, blocks, m):
        probs, l = [], None
        for sn in blocks:
            p = jnp.exp2(sn - m)
            ln = jnp.sum(p, axis=0, keepdims=True)
            l = ln if l is None else l + ln
            probs.append(p.astype(_BF16))
        p = jnp.concatenate(probs, axis=0) if jj > 0 else probs[0]
        vrows = slice(hh * HEAD_DIM, (hh + 1) * HEAD_DIM)
        o = jnp.dot(vT_ref[0, vrows, 0:(jj + 1) * BLK], p, preferred_element_type=_F32)
        tiles.setdefault(jj, []).append(
            _head_rms(o / l, cols_ref[vrows, COL_B_G:COL_B_G + 1]))
        if len(tiles[jj]) == HEADS_PER_STEP:
            y = jnp.concatenate(tiles.pop(jj), axis=0)
            store_tile(jj, y.T.astype(_BF16))

    jobs = [(jj, hh) for jj in range(nb) for hh in range(HEADS_PER_STEP)]
    return jobs, scores, finish


def _attn_kernel(qT_ref, k_ref, vT_ref, km_ref, cols_ref, yb_ref):
    def store_tile(jj, tile):
        yb_ref[0, jj * BLK:(jj + 1) * BLK, :] = tile

    jobs, scores, finish = _attn_jobs(qT_ref, k_ref, vT_ref, km_ref, cols_ref, store_tile)
    pending = []
    for job in jobs:
        pending.append((*job, *scores(*job)))
        if len(pending) > SCORES_AHEAD:
            finish(*pending.pop(0))
    for job in pending:
        finish(*job)


def _ffn_stages(load_x, load_y, wo_ref, g2_ref, wg_ref, wup_ref, wd_ref, store):
    st = {}

    def outproj():
        st["x1"] = load_x() + jnp.dot(load_y(), wo_ref[...], preferred_element_type=_F32)

    def norm():
        x1 = st["x1"]
        ms = jnp.mean(x1 * x1, axis=-1, keepdims=True)
        st["h"] = (x1 * lax.rsqrt(ms + EPS) * g2_ref[...]).astype(_BF16)

    def gate():
        st["g"] = jnp.dot(st["h"], wg_ref[...], preferred_element_type=_F32)

    def up():
        st["u"] = jnp.dot(st.pop("h"), wup_ref[...], preferred_element_type=_F32)

    def act():
        st["a"] = (jax.nn.silu(st.pop("g")) * st.pop("u")).astype(_BF16)

    def down():
        store(st.pop("x1") + jnp.dot(st.pop("a"), wd_ref[...], preferred_element_type=_F32))

    return outproj, norm, gate, up, act, down


def _ffn_kernel(x_ref, ya_ref, yb_ref, wo_ref, g2_ref, wg_ref, wup_ref, wd_ref, _aliased_out, o_ref):
    def store(val):
        o_ref[0] = val

    for stage in _ffn_stages(lambda: x_ref[0],
                             lambda: jnp.concatenate([ya_ref[0], yb_ref[0]], axis=1),
                             wo_ref, g2_ref, wg_ref, wup_ref, wd_ref, store):
        stage()


def _attn_ffn_kernel(qT_ref, k_ref, vT_ref, km_ref, cols_ref,
                     x_ref, ya_ref, yb0_ref, wo_ref, g2_ref, wg_ref, wup_ref, wd_ref,
                     yb_ref, o_ref, yb_scr):
    r, p = pl.program_id(0), pl.program_id(1)
    n_pair = yb_scr.shape[1]

    @pl.when((r == 0) & (p == 0))
    def _():
        for q in range(n_pair):
            yb_scr[0, q] = yb0_ref[0, :, q * PAIR:(q + 1) * PAIR]

    rd = r % 2
    wr = 1 - rd

    def store_tile(jj, tile):
        yb_ref[0, jj * BLK:(jj + 1) * BLK, :] = tile
        yb_scr[wr, p, jj * BLK:(jj + 1) * BLK, :] = tile

    jobs, scores, finish = _attn_jobs(qT_ref, k_ref, vT_ref, km_ref, cols_ref, store_tile)

    subs = []
    for t in range(TM3 // SUB3):
        rows = slice(t * SUB3, (t + 1) * SUB3)

        def load_y(t=t, rows=rows):
            tok = pl.ds(pl.multiple_of(p * TM3 + t * SUB3, SUB3), SUB3)
            return jnp.concatenate(
                [ya_ref[0, rows, :]] + [yb_scr[rd, q, tok, :] for q in range(n_pair)], axis=1)

        def store(val, rows=rows):
            o_ref[0, rows, :] = val

        subs.append(_ffn_stages(lambda rows=rows: x_ref[0, rows, :], load_y,
                                wo_ref, g2_ref, wg_ref, wup_ref, wd_ref, store))
    (a_out, a_norm, a_gate, a_up, a_act, a_down), (b_out, b_norm, b_gate, b_up, b_act, b_down) = subs
    slots = [(a_out,), (b_out, a_norm), (a_gate, b_norm), (a_up, a_act),
             (b_gate,), (b_up, b_act), (a_down,), (b_down,)]
    per_slot = len(jobs) // len(slots)
    assert per_slot * len(slots) == len(jobs) and per_slot % HEADS_PER_STEP == 0
    for i, slot in enumerate(slots):
        mine = jobs[i * per_slot:(i + 1) * per_slot]
        started = [(*job, *scores(*job)) for job in mine]
        for stage in slot:
            stage()
        for job in started:
            finish(*job)


def _full(shape):
    return pl.BlockSpec(shape, lambda *_: (0,) * len(shape), pipeline_mode=pl.Buffered(1))


def _layer(x, norm1_g, w_in, sgu_ln_g, sgu_ln_b, sgu_w, sgu_b, q_norm_g, k_norm_g,
           out_norm_a_g, out_norm_b_g, w_out, norm2_g, w_gate, w_up, w_down):
    B, S, D = x.shape
    nb = S // BLK
    assert D == D_MODEL and S % TM1 == 0

    w_in_t = w_in.astype(_BF16).T
    w_parts = [w_in_t[i * WIDTH:(i + 1) * WIDTH] for i in range(5)]
    cols = [None] * N_COLS
    cols[COL_LN_G], cols[COL_LN_B] = sgu_ln_g, sgu_ln_b
    cols[COL_A_G], cols[COL_B_G] = out_norm_a_g, out_norm_b_g
    cols[COL_Q_G], cols[COL_K_G] = jnp.tile(q_norm_g, N_HEADS), jnp.tile(k_norm_g, N_HEADS)
    cols = jnp.stack(cols, axis=1).astype(_F32)

    steps1 = B * (S // TM1)
    ffn_w = (w_out, w_gate, w_up, w_down)
    assert all(w.shape[0] % (16 * steps1) == 0 for w in ffn_w)
    slab_specs = [pl.BlockSpec((w.shape[0] // steps1, w.shape[1]), lambda b, s: (b * (S // TM1) + s, 0))
                  for w in ffn_w]

    ya, qT, k, vT, km, wo_b, wg_b, wup_b, wd_b = pl.pallas_call(
        _inproj_kernel,
        grid=(B, S // TM1),
        in_specs=[
            pl.BlockSpec((1, TM1, D), lambda b, s: (b, s, 0)),
            _full((1, D)),
            *[_full((WIDTH, D))] * 5,
            _full((WIDTH, N_COLS)),
            _full((N_HEADS, CHUNK, CHUNK)), _full((N_HEADS, CHUNK)),
            *slab_specs,
        ],
        out_specs=[
            pl.BlockSpec((1, TM1, WIDTH), lambda b, s: (b, s, 0)),
            pl.BlockSpec((1, WIDTH, TM1), lambda b, s: (b, 0, s)),
            pl.BlockSpec((1, TM1, WIDTH), lambda b, s: (b, s, 0)),
            pl.BlockSpec((1, WIDTH, TM1), lambda b, s: (b, 0, s)),
            pl.BlockSpec((1, nb, WIDTH), lambda b, s: (b, 0, 0)),
            *slab_specs,
        ],
        out_shape=[
            jax.ShapeDtypeStruct((B, S, WIDTH), _BF16),
            jax.ShapeDtypeStruct((B, WIDTH, S), _BF16),
            jax.ShapeDtypeStruct((B, S, WIDTH), _BF16),
            jax.ShapeDtypeStruct((B, WIDTH, S), _BF16),
            jax.ShapeDtypeStruct((B, nb, WIDTH), _F32),
            *[jax.ShapeDtypeStruct(w.shape, _BF16) for w in ffn_w],
        ],
        compiler_params=pltpu.CompilerParams(
            dimension_semantics=("parallel", "arbitrary"),
            vmem_limit_bytes=VMEM_LIMIT_BYTES),
        name="inproj_sgu",
    )(x, norm1_g.reshape(1, D), *w_parts, cols, sgu_w, sgu_b, *ffn_w)

    n_pair = WIDTH // PAIR
    assert S == n_pair * TM3 and B >= 2

    def attn_specs(row):
        return [
            pl.BlockSpec((1, PAIR, S), lambda r, p: (row(r), p, 0)),
            pl.BlockSpec((1, S, PAIR), lambda r, p: (row(r), 0, p)),
            pl.BlockSpec((1, PAIR, S), lambda r, p: (row(r), p, 0)),
            pl.BlockSpec((1, nb, PAIR), lambda r, p: (row(r), 0, p)),
            pl.BlockSpec((PAIR, N_COLS), lambda r, p: (p, 0)),
        ]

    ffn_w_specs = [_full((2 * WIDTH, D)), _full((1, D)), _full((D, D_FF)), _full((D, D_FF)), _full((D_FF, D))]
    ffn_w_args = (wo_b, norm2_g.reshape(1, D), wg_b, wup_b, wd_b)

    yb0 = pl.pallas_call(
        _attn_kernel,
        grid=(1, n_pair),
        in_specs=attn_specs(lambda r: r),
        out_specs=pl.BlockSpec((1, S, PAIR), lambda r, p: (r, 0, p)),
        out_shape=jax.ShapeDtypeStruct((1, S, WIDTH), _BF16),
        compiler_params=pltpu.CompilerParams(
            dimension_semantics=("parallel", "parallel"),
            vmem_limit_bytes=VMEM_LIMIT_BYTES),
        name="moba_attn",
    )(qT, k, vT, km, cols)

    yb, out = pl.pallas_call(
        _attn_ffn_kernel,
        grid=(B - 1, n_pair),
        in_specs=[
            *attn_specs(lambda r: r + 1),
            pl.BlockSpec((1, TM3, D), lambda r, p: (r, p, 0)),
            pl.BlockSpec((1, TM3, WIDTH), lambda r, p: (r, p, 0)),
            _full((1, S, WIDTH)),
            *ffn_w_specs,
        ],
        out_specs=[
            pl.BlockSpec((1, S, PAIR), lambda r, p: (r + 1, 0, p)),
            pl.BlockSpec((1, TM3, D), lambda r, p: (r, p, 0)),
        ],
        out_shape=[
            jax.ShapeDtypeStruct((B, S, WIDTH), _BF16),
            jax.ShapeDtypeStruct((B, S, D), _F32),
        ],
        scratch_shapes=[pltpu.VMEM((2, n_pair, S, PAIR), _BF16)],
        compiler_params=pltpu.CompilerParams(
            dimension_semantics=("arbitrary", "arbitrary"),
            vmem_limit_bytes=VMEM_LIMIT_BYTES),
        name="attn_ffn",
    )(qT, k, vT, km, cols, x, ya, yb0, *ffn_w_args)

    last = B - 1
    out = pl.pallas_call(
        _ffn_kernel,
        grid=(n_pair,),
        in_specs=[
            pl.BlockSpec((1, TM3, D), lambda i: (last, i, 0)),
            pl.BlockSpec((1, TM3, WIDTH), lambda i: (last, i, 0)),
            pl.BlockSpec((1, TM3, WIDTH), lambda i: (last, i, 0)),
            *ffn_w_specs,
            pl.BlockSpec(memory_space=pl.ANY),
        ],
        out_specs=pl.BlockSpec((1, TM3, D), lambda i: (last, i, 0)),
        out_shape=jax.ShapeDtypeStruct((B, S, D), _F32),
        input_output_aliases={8: 0},
        compiler_params=pltpu.CompilerParams(
            dimension_semantics=("parallel",),
            vmem_limit_bytes=VMEM_LIMIT_BYTES),
        name="outproj_ffn",
    )(x, ya, yb, *ffn_w_args, out)
    return out


def kernel(x, norm1_g, w_in, sgu_ln_g, sgu_ln_b, sgu_w, sgu_b, q_norm_g, k_norm_g,
           out_norm_a_g, out_norm_b_g, w_out, norm2_g, w_gate, w_up, w_down):
    depth = norm1_g.shape[0]
    for l in range(depth):
        x = _layer(x, norm1_g[l], w_in[l], sgu_ln_g[l], sgu_ln_b[l], sgu_w[l], sgu_b[l],
                   q_norm_g[l], k_norm_g[l], out_norm_a_g[l], out_norm_b_g[l], w_out[l],
                   norm2_g[l], w_gate[l], w_up[l], w_down[l])
    return x
```

```python
import jax
import jax.numpy as jnp
from jax import lax
from jax.experimental import pallas as pl
from jax.experimental.pallas import tpu as pltpu

D_MODEL = 1024
HEAD_DIM = 64
N_HEADS = 8
WIDTH = N_HEADS * HEAD_DIM
CHUNK = 128
BLK = 256
TOPK = 3
D_FF = 2816
EPS = 1e-6
Q_SCALE = HEAD_DIM ** -0.5 * 1.4426950408889634

TM1 = 1024
SUB1 = 256
MIX_HEADS = 2
TM3 = 512
SUB3 = 256
HEADS_PER_STEP = 2
PAIR = HEADS_PER_STEP * HEAD_DIM
SCORES_AHEAD = 2

COL_LN_G, COL_LN_B, COL_A_G, COL_B_G, COL_Q_G, COL_K_G = range(6)
N_COLS = 6

VMEM_LIMIT_BYTES = 56 * 1024 * 1024
MXU_TILE_V7X = 256

_NT = (((1,), (1,)), ((), ()))
_F32 = jnp.float32
_BF16 = jnp.bfloat16


def _head_rms(t, gain):
    ms = jnp.mean(t * t, axis=0, keepdims=True)
    return t * lax.rsqrt(ms + EPS) * gain


def _inproj_kernel(x_ref, g1_ref, win_ref, cols_ref, sw_ref, sb_ref,
                   wo32_ref, wg32_ref, wup32_ref, wd32_ref,
                   ya_ref, qT_ref, k_ref, vT_ref, km_ref, wo_ref, wg_ref, wup_ref, wd_ref):
    for src, dst in ((wo32_ref, wo_ref), (wg32_ref, wg_ref), (wup32_ref, wup_ref), (wd32_ref, wd_ref)):
        dst[...] = src[...].astype(_BF16)

    si = pl.program_id(1)
    nsub = TM1 // SUB1
    row = lax.broadcasted_iota(jnp.int32, (CHUNK, CHUNK), 0)
    col = lax.broadcasted_iota(jnp.int32, (CHUNK, CHUNK), 1)
    causal = col <= row
    w_mix = [jnp.where(causal, sw_ref[hd], 0.0).astype(_BF16) for hd in range(N_HEADS)]
    w_mix = [jnp.concatenate(w_mix[h0:h0 + MIX_HEADS], axis=0)
             for h0 in range(0, N_HEADS, MIX_HEADS)]

    def project(t):
        x = x_ref[0, t * SUB1:(t + 1) * SUB1, :]
        ms = jnp.mean(x * x, axis=-1, keepdims=True)
        h = (x * lax.rsqrt(ms + EPS) * g1_ref[...]).astype(_BF16)
        return [lax.dot_general(win_ref[i * WIDTH:(i + 1) * WIDTH, :], h, _NT, preferred_element_type=_F32)
                for i in range(5)]

    pending = []
    for t in range(nsub):
        pending.append((t, project(t)))
        if len(pending) > 1:
            _inproj_finish(*pending.pop(0), si * nsub, w_mix, cols_ref, sb_ref,
                           ya_ref, qT_ref, k_ref, vT_ref, km_ref)
    _inproj_finish(*pending.pop(0), si * nsub, w_mix, cols_ref, sb_ref,
                   ya_ref, qT_ref, k_ref, vT_ref, km_ref)


def _inproj_finish(t, projs, blk0, w_mix, cols_ref, sb_ref, ya_ref, qT_ref, k_ref, vT_ref, km_ref):
    pu, pv, qt, kt, vbt = projs
    nchunk = SUB1 // CHUNK
    rows = slice(t * SUB1, (t + 1) * SUB1)

    u = jax.nn.gelu(pu)
    v = jax.nn.gelu(pv)
    ya = []
    per_head = nchunk * HEAD_DIM
    for h0 in range(0, N_HEADS, MIX_HEADS):
        heads = range(h0, h0 + MIX_HEADS)
        lhs = []
        for hd in heads:
            sl = slice(hd * HEAD_DIM, (hd + 1) * HEAD_DIM)
            vh = v[sl, :]
            mu = jnp.mean(vh, axis=0, keepdims=True)
            vc = vh - mu
            var = jnp.mean(vc * vc, axis=0, keepdims=True)
            vn = (vc * lax.rsqrt(var + EPS) * cols_ref[sl, COL_LN_G:COL_LN_G + 1]
                  + cols_ref[sl, COL_LN_B:COL_LN_B + 1])
            lhs += [vn[:, c * CHUNK:(c + 1) * CHUNK] for c in range(nchunk)]
        lhs = jnp.concatenate(lhs, axis=0).astype(_BF16)
        mixed_all = lax.dot_general(lhs, w_mix[h0 // MIX_HEADS], _NT, preferred_element_type=_F32)
        for i, hd in enumerate(heads):
            sl = slice(hd * HEAD_DIM, (hd + 1) * HEAD_DIM)
            mixed = mixed_all[i * per_head:(i + 1) * per_head, i * CHUNK:(i + 1) * CHUNK]
            mixed = mixed + sb_ref[hd:hd + 1, :]
            mixed = jnp.concatenate(
                [mixed[c * HEAD_DIM:(c + 1) * HEAD_DIM, :] for c in range(nchunk)], axis=1)
            ya.append(_head_rms(u[sl, :] * mixed, cols_ref[sl, COL_A_G:COL_A_G + 1]))
    ya_ref[0, rows, :] = jnp.concatenate(ya, axis=0).T.astype(_BF16)

    qn = jnp.concatenate(
        [_head_rms(qt[hd * HEAD_DIM:(hd + 1) * HEAD_DIM, :], cols_ref[0:HEAD_DIM, COL_Q_G:COL_Q_G + 1])
         for hd in range(N_HEADS)], axis=0) * Q_SCALE
    kn = jnp.concatenate(
        [_head_rms(kt[hd * HEAD_DIM:(hd + 1) * HEAD_DIM, :], cols_ref[0:HEAD_DIM, COL_K_G:COL_K_G + 1])
         for hd in range(N_HEADS)], axis=0)
    k_tok = kn.T
    qT_ref[0, :, rows] = qn.astype(_BF16)
    vT_ref[0, :, rows] = vbt.astype(_BF16)
    k_ref[0, rows, :] = k_tok.astype(_BF16)
    for b in range(SUB1 // BLK):
        kb = k_tok[b * BLK:(b + 1) * BLK, :]
        km_ref[0, pl.ds((blk0 + t) * (SUB1 // BLK) + b, 1), :] = jnp.mean(kb, axis=0, keepdims=True)


def _attn_jobs(qT_ref, k_ref, vT_ref, km_ref, cols_ref, kx_scr, store_tile):
    nb = k_ref.shape[1] // BLK
    frow = lax.broadcasted_iota(jnp.int32, (PAIR, BLK), 0)
    key_pos = lax.broadcasted_iota(jnp.int32, (BLK, BLK), 0)
    qry_pos = lax.broadcasted_iota(jnp.int32, (BLK, BLK), 1)
    causal_bias = jnp.where(key_pos <= qry_pos, 0.0, -jnp.inf)

    km = km_ref[0]
    km_hi = km.astype(_BF16)
    km_lo = (km - km_hi.astype(_F32)).astype(_BF16)
    kx_scr[0:2 * nb, :] = jnp.concatenate([km_hi, km_lo], axis=0)
    kx_scr[2 * nb:, :] = k_ref[0]

    def scores(jj, hh):
        nkeys = (jj + 1) * BLK
        qT = qT_ref[0, :, jj * BLK:(jj + 1) * BLK]
        in_head = (frow >= hh * HEAD_DIM) & (frow < (hh + 1) * HEAD_DIM)
        qm = jnp.where(in_head, qT, jnp.zeros_like(qT))
        bias = None
        if jj > TOPK:
            sx = jnp.dot(kx_scr[0:2 * nb + nkeys, :], qm, preferred_element_type=_F32)
            gate = sx[0:jj, :] + sx[nb:nb + jj, :]
            s = sx[2 * nb:, :]
            blk_id = lax.broadcasted_iota(jnp.int32, (jj, BLK), 0)
            rank = jnp.zeros((jj, BLK), _F32)
            for m in range(jj):
                gm = gate[m:m + 1, :]
                ahead = (gm > gate) | ((gm == gate) & (blk_id > m))
                rank = rank + jnp.where(ahead, 1.0, 0.0)
            bias = jnp.where(rank < TOPK, 0.0, -jnp.inf)
        else:
            s = jnp.dot(kx_scr[2 * nb:2 * nb + nkeys, :], qm, preferred_element_type=_F32)
        blocks, m = [], None
        for n in range(jj + 1):
            sn = s[n * BLK:(n + 1) * BLK, :]
            if n == jj:
                sn = sn + causal_bias
            elif bias is not None:
                sn = sn + bias[n:n + 1, :]
            blocks.append(sn)
            mn = jnp.max(sn, axis=0, keepdims=True)
            m = mn if m is None else jnp.maximum(m, mn)
        return blocks, m

    tiles = {}

    def softmax(jj, hh, blocks, m):
        probs, l = [], None
        for sn in blocks:
            p = jnp.exp2(sn - m)
            ln = jnp.sum(p, axis=0, keepdims=True)
            l = ln if l is None else l + ln
            probs.append(p.astype(_BF16))
        p = jnp.concatenate(probs, axis=0) if jj > 0 else probs[0]
        return jj, hh, p, l

    def pv(jj, hh, p, l):
        vrows = slice(hh * HEAD_DIM, (hh + 1) * HEAD_DIM)
        o = jnp.dot(vT_ref[0, vrows, 0:(jj + 1) * BLK], p, preferred_element_type=_F32)
        tiles.setdefault(jj, []).append(
            _head_rms(o / l, cols_ref[vrows, COL_B_G:COL_B_G + 1]))
        if len(tiles[jj]) == HEADS_PER_STEP:
            y = jnp.concatenate(tiles.pop(jj), axis=0)
            store_tile(jj, y.T.astype(_BF16))

    jobs = [(jj, hh) for jj in range(nb) for hh in range(HEADS_PER_STEP)]
    return jobs, scores, softmax, pv


def _attn_kernel(qT_ref, k_ref, vT_ref, km_ref, cols_ref, yb_ref, kx_scr):
    def store_tile(jj, tile):
        yb_ref[0, jj * BLK:(jj + 1) * BLK, :] = tile

    jobs, scores, softmax, pv = _attn_jobs(qT_ref, k_ref, vT_ref, km_ref, cols_ref, kx_scr, store_tile)
    pending = []
    for job in jobs:
        pending.append((*job, *scores(*job)))
        if len(pending) > SCORES_AHEAD:
            pv(*softmax(*pending.pop(0)))
    for job in pending:
        pv(*softmax(*job))


def _ffn_stages(load_x, load_y, wo_ref, g2_ref, wg_ref, wup_ref, wd_ref, store):
    st = {}

    def outproj():
        st["x1"] = load_x() + jnp.dot(load_y(), wo_ref[...], preferred_element_type=_F32)

    def norm():
        x1 = st["x1"]
        ms = jnp.mean(x1 * x1, axis=-1, keepdims=True)
        st["h"] = (x1 * lax.rsqrt(ms + EPS) * g2_ref[...]).astype(_BF16)

    def gate():
        st["g"] = jnp.dot(st["h"], wg_ref[...], preferred_element_type=_F32)

    def up():
        st["u"] = jnp.dot(st.pop("h"), wup_ref[...], preferred_element_type=_F32)

    def act():
        st["a"] = (jax.nn.silu(st.pop("g")) * st.pop("u")).astype(_BF16)

    def down():
        store(st.pop("x1") + jnp.dot(st.pop("a"), wd_ref[...], preferred_element_type=_F32))

    return outproj, norm, gate, up, act, down


def _ffn_kernel(x_ref, ya_ref, yb_ref, wo_ref, g2_ref, wg_ref, wup_ref, wd_ref, _aliased_out, o_ref):
    def store(val):
        o_ref[0] = val

    for stage in _ffn_stages(lambda: x_ref[0],
                             lambda: jnp.concatenate([ya_ref[0], yb_ref[0]], axis=1),
                             wo_ref, g2_ref, wg_ref, wup_ref, wd_ref, store):
        stage()


def _attn_ffn_kernel(qT_ref, k_ref, vT_ref, km_ref, cols_ref,
                     x_ref, ya_ref, yb0_ref, wo_ref, g2_ref, wg_ref, wup_ref, wd_ref,
                     yb_ref, o_ref, kx_scr, yb_scr):
    r, p = pl.program_id(0), pl.program_id(1)
    n_pair = yb_scr.shape[1]

    @pl.when((r == 0) & (p == 0))
    def _():
        for q in range(n_pair):
            yb_scr[0, q] = yb0_ref[0, :, q * PAIR:(q + 1) * PAIR]

    rd = r % 2
    wr = 1 - rd

    def store_tile(jj, tile):
        yb_ref[0, jj * BLK:(jj + 1) * BLK, :] = tile
        yb_scr[wr, p, jj * BLK:(jj + 1) * BLK, :] = tile

    jobs, scores, softmax, pv = _attn_jobs(qT_ref, k_ref, vT_ref, km_ref, cols_ref, kx_scr, store_tile)

    subs = []
    for t in range(TM3 // SUB3):
        rows = slice(t * SUB3, (t + 1) * SUB3)

        def load_y(t=t, rows=rows):
            tok = pl.ds(pl.multiple_of(p * TM3 + t * SUB3, SUB3), SUB3)
            return jnp.concatenate(
                [ya_ref[0, rows, :]] + [yb_scr[rd, q, tok, :] for q in range(n_pair)], axis=1)

        def store(val, rows=rows):
            o_ref[0, rows, :] = val

        subs.append(_ffn_stages(lambda rows=rows: x_ref[0, rows, :], load_y,
                                wo_ref, g2_ref, wg_ref, wup_ref, wd_ref, store))
    (a_out, a_norm, a_gate, a_up, a_act, a_down), (b_out, b_norm, b_gate, b_up, b_act, b_down) = subs
    slots = [(a_out,), (b_out, a_norm), (a_gate, b_norm), (a_up, a_act),
             (b_gate,), (b_up, b_act), (a_down,), (b_down,)]
    mxu_tiles = lambda k, n: (k // MXU_TILE_V7X) * (n // MXU_TILE_V7X)
    room = [mxu_tiles(D_MODEL, D_MODEL)] * 2 + [mxu_tiles(D_MODEL, D_FF)] * 6
    unit = sum(room) / sum(jj + 1 for jj, _ in jobs)
    room = [r / unit for r in room]
    slot_jobs = [[] for _ in slots]
    for job in sorted(jobs, key=lambda job: -job[0]):
        i = max(range(len(slots)), key=lambda i: room[i])
        slot_jobs[i].append(job)
        room[i] -= job[0] + 1
    for slot, mine in zip(slots, slot_jobs):
        started = [(*job, *scores(*job)) for job in mine]
        weighted = [softmax(*job) for job in started]
        for stage in slot:
            stage()
        for job in weighted:
            pv(*job)


def _full(shape):
    return pl.BlockSpec(shape, lambda *_: (0,) * len(shape), pipeline_mode=pl.Buffered(1))


def _layer(x, norm1_g, w_in, sgu_ln_g, sgu_ln_b, sgu_w, sgu_b, q_norm_g, k_norm_g,
           out_norm_a_g, out_norm_b_g, w_out, norm2_g, w_gate, w_up, w_down):
    B, S, D = x.shape
    nb = S // BLK
    assert D == D_MODEL and S % TM1 == 0

    w_in_t = w_in.astype(_BF16).T
    cols = [None] * N_COLS
    cols[COL_LN_G], cols[COL_LN_B] = sgu_ln_g, sgu_ln_b
    cols[COL_A_G], cols[COL_B_G] = out_norm_a_g, out_norm_b_g
    cols[COL_Q_G], cols[COL_K_G] = jnp.tile(q_norm_g, N_HEADS), jnp.tile(k_norm_g, N_HEADS)
    cols = jnp.stack(cols, axis=1).astype(_F32)

    steps1 = B * (S // TM1)
    ffn_w = (w_out, w_gate, w_up, w_down)
    assert all(w.shape[0] % (16 * steps1) == 0 for w in ffn_w)
    slab_specs = [pl.BlockSpec((w.shape[0] // steps1, w.shape[1]), lambda b, s: (b * (S // TM1) + s, 0))
                  for w in ffn_w]

    ya, qT, k, vT, km, wo_b, wg_b, wup_b, wd_b = pl.pallas_call(
        _inproj_kernel,
        grid=(B, S // TM1),
        in_specs=[
            pl.BlockSpec((1, TM1, D), lambda b, s: (b, s, 0)),
            _full((1, D)),
            _full(w_in_t.shape),
            _full((WIDTH, N_COLS)),
            _full((N_HEADS, CHUNK, CHUNK)), _full((N_HEADS, CHUNK)),
            *slab_specs,
        ],
        out_specs=[
            pl.BlockSpec((1, TM1, WIDTH), lambda b, s: (b, s, 0)),
            pl.BlockSpec((1, WIDTH, TM1), lambda b, s: (b, 0, s)),
            pl.BlockSpec((1, TM1, WIDTH), lambda b, s: (b, s, 0)),
            pl.BlockSpec((1, WIDTH, TM1), lambda b, s: (b, 0, s)),
            pl.BlockSpec((1, nb, WIDTH), lambda b, s: (b, 0, 0)),
            *slab_specs,
        ],
        out_shape=[
            jax.ShapeDtypeStruct((B, S, WIDTH), _BF16),
            jax.ShapeDtypeStruct((B, WIDTH, S), _BF16),
            jax.ShapeDtypeStruct((B, S, WIDTH), _BF16),
            jax.ShapeDtypeStruct((B, WIDTH, S), _BF16),
            jax.ShapeDtypeStruct((B, nb, WIDTH), _F32),
            *[jax.ShapeDtypeStruct(w.shape, _BF16) for w in ffn_w],
        ],
        compiler_params=pltpu.CompilerParams(
            dimension_semantics=("parallel", "arbitrary"),
            vmem_limit_bytes=VMEM_LIMIT_BYTES),
        name="inproj_sgu",
    )(x, norm1_g.reshape(1, D), w_in_t, cols, sgu_w, sgu_b, *ffn_w)

    n_pair = WIDTH // PAIR
    assert S == n_pair * TM3 and B >= 2

    def attn_specs(row):
        return [
            pl.BlockSpec((1, PAIR, S), lambda r, p: (row(r), p, 0)),
            pl.BlockSpec((1, S, PAIR), lambda r, p: (row(r), 0, p)),
            pl.BlockSpec((1, PAIR, S), lambda r, p: (row(r), p, 0)),
            pl.BlockSpec((1, nb, PAIR), lambda r, p: (row(r), 0, p)),
            pl.BlockSpec((PAIR, N_COLS), lambda r, p: (p, 0)),
        ]

    kx_scratch = pltpu.VMEM((2 * nb + S, PAIR), _BF16)
    ffn_w_specs = [_full((2 * WIDTH, D)), _full((1, D)), _full((D, D_FF)), _full((D, D_FF)), _full((D_FF, D))]
    ffn_w_args = (wo_b, norm2_g.reshape(1, D), wg_b, wup_b, wd_b)

    yb0 = pl.pallas_call(
        _attn_kernel,
        grid=(1, n_pair),
        in_specs=attn_specs(lambda r: r),
        out_specs=pl.BlockSpec((1, S, PAIR), lambda r, p: (r, 0, p)),
        out_shape=jax.ShapeDtypeStruct((1, S, WIDTH), _BF16),
        scratch_shapes=[kx_scratch],
        compiler_params=pltpu.CompilerParams(
            dimension_semantics=("parallel", "parallel"),
            vmem_limit_bytes=VMEM_LIMIT_BYTES),
        name="moba_attn",
    )(qT, k, vT, km, cols)

    yb, out = pl.pallas_call(
        _attn_ffn_kernel,
        grid=(B - 1, n_pair),
        in_specs=[
            *attn_specs(lambda r: r + 1),
            pl.BlockSpec((1, TM3, D), lambda r, p: (r, p, 0)),
            pl.BlockSpec((1, TM3, WIDTH), lambda r, p: (r, p, 0)),
            _full((1, S, WIDTH)),
            *ffn_w_specs,
        ],
        out_specs=[
            pl.BlockSpec((1, S, PAIR), lambda r, p: (r + 1, 0, p)),
            pl.BlockSpec((1, TM3, D), lambda r, p: (r, p, 0)),
        ],
        out_shape=[
            jax.ShapeDtypeStruct((B, S, WIDTH), _BF16),
            jax.ShapeDtypeStruct((B, S, D), _F32),
        ],
        scratch_shapes=[kx_scratch, pltpu.VMEM((2, n_pair, S, PAIR), _BF16)],
        compiler_params=pltpu.CompilerParams(
            dimension_semantics=("arbitrary", "arbitrary"),
            vmem_limit_bytes=VMEM_LIMIT_BYTES),
        name="attn_ffn",
    )(qT, k, vT, km, cols, x, ya, yb0, *ffn_w_args)

    last = B - 1
    out = pl.pallas_call(
        _ffn_kernel,
        grid=(n_pair,),
        in_specs=[
            pl.BlockSpec((1, TM3, D), lambda i: (last, i, 0)),
            pl.BlockSpec((1, TM3, WIDTH), lambda i: (last, i, 0)),
            pl.BlockSpec((1, TM3, WIDTH), lambda i: (last, i, 0)),
            *ffn_w_specs,
            pl.BlockSpec(memory_space=pl.ANY),
        ],
        out_specs=pl.BlockSpec((1, TM3, D), lambda i: (last, i, 0)),
        out_shape=jax.ShapeDtypeStruct((B, S, D), _F32),
        input_output_aliases={8: 0},
        compiler_params=pltpu.CompilerParams(
            dimension_semantics=("parallel",),
            vmem_limit_bytes=VMEM_LIMIT_BYTES),
        name="outproj_ffn",
    )(x, ya, yb, *ffn_w_args, out)
    return out


def kernel(x, norm1_g, w_in, sgu_ln_g, sgu_ln_b, sgu_w, sgu_b, q_norm_g, k_norm_g,
           out_norm_a_g, out_norm_b_g, w_out, norm2_g, w_gate, w_up, w_down):
    depth = norm1_g.shape[0]
    for l in range(depth):
        x = _layer(x, norm1_g[l], w_in[l], sgu_ln_g[l], sgu_ln_b[l], sgu_w[l], sgu_b[l],
                   q_norm_g[l], k_norm_g[l], out_norm_a_g[l], out_norm_b_g[l], w_out[l],
                   norm2_g[l], w_gate[l], w_up[l], w_down[l])
    return x
```

```python
import jax
import jax.numpy as jnp
from jax import lax
from jax.experimental import pallas as pl
from jax.experimental.pallas import tpu as pltpu

D_MODEL = 1024
HEAD_DIM = 64
N_HEADS = 8
WIDTH = N_HEADS * HEAD_DIM
CHUNK = 128
BLK = 256
TOPK = 3
D_FF = 2816
EPS = 1e-6
Q_SCALE = HEAD_DIM ** -0.5 * 1.4426950408889634

TM1 = 1024
SUB1 = 256
MIX_HEADS = 2
TM3 = 512
SUB3 = 256
HEADS_PER_STEP = 2
PAIR = HEADS_PER_STEP * HEAD_DIM
SCORES_AHEAD = 2

COL_LN_G, COL_LN_B, COL_A_G, COL_B_G, COL_Q_G, COL_K_G = range(6)
N_COLS = 6

VMEM_LIMIT_BYTES = 56 * 1024 * 1024
_NT = (((1,), (1,)), ((), ()))
_F32 = jnp.float32
_BF16 = jnp.bfloat16


def _head_rms(t, gain):
    ms = jnp.mean(t * t, axis=0, keepdims=True)
    return t * lax.rsqrt(ms + EPS) * gain


def _inproj_kernel(x_ref, g1_ref, win_ref, cols_ref, sw_ref, sb_ref,
                   wo32_ref, wg32_ref, wup32_ref, wd32_ref,
                   ya_ref, qT_ref, k_ref, vT_ref, km_ref, wo_ref, wg_ref, wup_ref, wd_ref):
    for src, dst in ((wo32_ref, wo_ref), (wg32_ref, wg_ref), (wup32_ref, wup_ref), (wd32_ref, wd_ref)):
        dst[...] = src[...].astype(_BF16)

    si = pl.program_id(1)
    nsub = TM1 // SUB1
    row = lax.broadcasted_iota(jnp.int32, (CHUNK, CHUNK), 0)
    col = lax.broadcasted_iota(jnp.int32, (CHUNK, CHUNK), 1)
    causal = col <= row
    w_mix = [jnp.where(causal, sw_ref[hd], 0.0).astype(_BF16) for hd in range(N_HEADS)]
    w_mix = [jnp.concatenate(w_mix[h0:h0 + MIX_HEADS], axis=0)
             for h0 in range(0, N_HEADS, MIX_HEADS)]

    def project(t):
        x = x_ref[0, t * SUB1:(t + 1) * SUB1, :]
        ms = jnp.mean(x * x, axis=-1, keepdims=True)
        h = (x * lax.rsqrt(ms + EPS) * g1_ref[...]).astype(_BF16)
        return [lax.dot_general(win_ref[i * WIDTH:(i + 1) * WIDTH, :], h, _NT, preferred_element_type=_F32)
                for i in range(5)]

    pending = []
    for t in range(nsub):
        pending.append((t, project(t)))
        if len(pending) > 1:
            _inproj_finish(*pending.pop(0), si * nsub, w_mix, cols_ref, sb_ref,
                           ya_ref, qT_ref, k_ref, vT_ref, km_ref)
    _inproj_finish(*pending.pop(0), si * nsub, w_mix, cols_ref, sb_ref,
                   ya_ref, qT_ref, k_ref, vT_ref, km_ref)


def _inproj_finish(t, projs, blk0, w_mix, cols_ref, sb_ref, ya_ref, qT_ref, k_ref, vT_ref, km_ref):
    pu, pv, qt, kt, vbt = projs
    nchunk = SUB1 // CHUNK
    rows = slice(t * SUB1, (t + 1) * SUB1)

    u = jax.nn.gelu(pu)
    v = jax.nn.gelu(pv)
    ya = []
    per_head = nchunk * HEAD_DIM
    for h0 in range(0, N_HEADS, MIX_HEADS):
        heads = range(h0, h0 + MIX_HEADS)
        lhs = []
        for hd in heads:
            sl = slice(hd * HEAD_DIM, (hd + 1) * HEAD_DIM)
            vh = v[sl, :]
            mu = jnp.mean(vh, axis=0, keepdims=True)
            vc = vh - mu
            var = jnp.mean(vc * vc, axis=0, keepdims=True)
            vn = (vc * lax.rsqrt(var + EPS) * cols_ref[sl, COL_LN_G:COL_LN_G + 1]
                  + cols_ref[sl, COL_LN_B:COL_LN_B + 1])
            lhs += [vn[:, c * CHUNK:(c + 1) * CHUNK] for c in range(nchunk)]
        lhs = jnp.concatenate(lhs, axis=0).astype(_BF16)
        mixed_all = lax.dot_general(lhs, w_mix[h0 // MIX_HEADS], _NT, preferred_element_type=_F32)
        for i, hd in enumerate(heads):
            sl = slice(hd * HEAD_DIM, (hd + 1) * HEAD_DIM)
            mixed = mixed_all[i * per_head:(i + 1) * per_head, i * CHUNK:(i + 1) * CHUNK]
            mixed = mixed + sb_ref[hd:hd + 1, :]
            mixed = jnp.concatenate(
                [mixed[c * HEAD_DIM:(c + 1) * HEAD_DIM, :] for c in range(nchunk)], axis=1)
            ya.append(_head_rms(u[sl, :] * mixed, cols_ref[sl, COL_A_G:COL_A_G + 1]))
    ya_ref[0, rows, :] = jnp.concatenate(ya, axis=0).T.astype(_BF16)

    qn = jnp.concatenate(
        [_head_rms(qt[hd * HEAD_DIM:(hd + 1) * HEAD_DIM, :], cols_ref[0:HEAD_DIM, COL_Q_G:COL_Q_G + 1])
         for hd in range(N_HEADS)], axis=0) * Q_SCALE
    kn = jnp.concatenate(
        [_head_rms(kt[hd * HEAD_DIM:(hd + 1) * HEAD_DIM, :], cols_ref[0:HEAD_DIM, COL_K_G:COL_K_G + 1])
         for hd in range(N_HEADS)], axis=0)
    k_tok = kn.T
    qT_ref[0, :, rows] = qn.astype(_BF16)
    vT_ref[0, :, rows] = vbt.astype(_BF16)
    k_ref[0, rows, :] = k_tok.astype(_BF16)
    for b in range(SUB1 // BLK):
        kb = k_tok[b * BLK:(b + 1) * BLK, :]
        km_ref[0, pl.ds((blk0 + t) * (SUB1 // BLK) + b, 1), :] = jnp.mean(kb, axis=0, keepdims=True)


def _attn_jobs(qT_ref, k_ref, vT_ref, km_ref, cols_ref, kx_scr, store_tile):
    nb = k_ref.shape[1] // BLK
    frow = lax.broadcasted_iota(jnp.int32, (PAIR, BLK), 0)
    key_pos = lax.broadcasted_iota(jnp.int32, (BLK, BLK), 0)
    qry_pos = lax.broadcasted_iota(jnp.int32, (BLK, BLK), 1)
    causal_bias = jnp.where(key_pos <= qry_pos, 0.0, -jnp.inf)

    km = km_ref[0]
    km_hi = km.astype(_BF16)
    km_lo = (km - km_hi.astype(_F32)).astype(_BF16)
    kx_scr[0:2 * nb, :] = jnp.concatenate([km_hi, km_lo], axis=0)
    kx_scr[2 * nb:, :] = k_ref[0]

    def scores(jj, hh):
        nkeys = (jj + 1) * BLK
        qT = qT_ref[0, :, jj * BLK:(jj + 1) * BLK]
        in_head = (frow >= hh * HEAD_DIM) & (frow < (hh + 1) * HEAD_DIM)
        qm = jnp.where(in_head, qT, jnp.zeros_like(qT))
        bias = None
        if jj > TOPK:
            sx = jnp.dot(kx_scr[0:2 * nb + nkeys, :], qm, preferred_element_type=_F32)
            gate = sx[0:jj, :] + sx[nb:nb + jj, :]
            s = sx[2 * nb:, :]
            blk_id = lax.broadcasted_iota(jnp.int32, (jj, BLK), 0)
            rank = jnp.zeros((jj, BLK), _F32)
            for m in range(jj):
                gm = gate[m:m + 1, :]
                ahead = (gm > gate) | ((gm == gate) & (blk_id > m))
                rank = rank + jnp.where(ahead, 1.0, 0.0)
            bias = jnp.where(rank < TOPK, 0.0, -jnp.inf)
        else:
            s = jnp.dot(kx_scr[2 * nb:2 * nb + nkeys, :], qm, preferred_element_type=_F32)
        blocks, m = [], None
        for n in range(jj + 1):
            sn = s[n * BLK:(n + 1) * BLK, :]
            if n == jj:
                sn = sn + causal_bias
            elif bias is not None:
                sn = sn + bias[n:n + 1, :]
            blocks.append(sn)
            mn = jnp.max(sn, axis=0, keepdims=True)
            m = mn if m is None else jnp.maximum(m, mn)
        return blocks, m

    tiles = {}

    def softmax(jj, hh, blocks, m):
        probs, l = [], None
        for sn in blocks:
            p = jnp.exp2(sn - m)
            ln = jnp.sum(p, axis=0, keepdims=True)
            l = ln if l is None else l + ln
            probs.append(p.astype(_BF16))
        p = jnp.concatenate(probs, axis=0) if jj > 0 else probs[0]
        return jj, hh, p, l

    def pv(jj, hh, p, l):
        vrows = slice(hh * HEAD_DIM, (hh + 1) * HEAD_DIM)
        o = jnp.dot(vT_ref[0, vrows, 0:(jj + 1) * BLK], p, preferred_element_type=_F32)
        tiles.setdefault(jj, []).append(
            _head_rms(o / l, cols_ref[vrows, COL_B_G:COL_B_G + 1]))
        if len(tiles[jj]) == HEADS_PER_STEP:
            y = jnp.concatenate(tiles.pop(jj), axis=0)
            store_tile(jj, y.T.astype(_BF16))

    jobs = [(jj, hh) for jj in range(nb) for hh in range(HEADS_PER_STEP)]
    return jobs, scores, softmax, pv


def _attn_kernel(qT_ref, k_ref, vT_ref, km_ref, cols_ref, yb_ref, kx_scr):
    def store_tile(jj, tile):
        yb_ref[0, jj * BLK:(jj + 1) * BLK, :] = tile

    jobs, scores, softmax, pv = _attn_jobs(qT_ref, k_ref, vT_ref, km_ref, cols_ref, kx_scr, store_tile)
    pending = []
    for job in jobs:
        pending.append((*job, *scores(*job)))
        if len(pending) > SCORES_AHEAD:
            pv(*softmax(*pending.pop(0)))
    for job in pending:
        pv(*softmax(*job))


def _ffn_stages(load_x, load_y, wo_ref, g2_ref, wg_ref, wup_ref, wd_ref, store):
    st = {}

    def outproj():
        st["x1"] = load_x() + jnp.dot(load_y(), wo_ref[...], preferred_element_type=_F32)

    def norm():
        x1 = st["x1"]
        ms = jnp.mean(x1 * x1, axis=-1, keepdims=True)
        st["h"] = (x1 * lax.rsqrt(ms + EPS) * g2_ref[...]).astype(_BF16)

    def gate():
        st["g"] = jnp.dot(st["h"], wg_ref[...], preferred_element_type=_F32)

    def up():
        st["u"] = jnp.dot(st.pop("h"), wup_ref[...], preferred_element_type=_F32)

    def act():
        st["a"] = (jax.nn.silu(st.pop("g")) * st.pop("u")).astype(_BF16)

    def down():
        store(st.pop("x1") + jnp.dot(st.pop("a"), wd_ref[...], preferred_element_type=_F32))

    return outproj, norm, gate, up, act, down


def _ffn_kernel(x_ref, ya_ref, yb_ref, wo_ref, g2_ref, wg_ref, wup_ref, wd_ref, _aliased_out, o_ref):
    def store(val):
        o_ref[0] = val

    for stage in _ffn_stages(lambda: x_ref[0],
                             lambda: jnp.concatenate([ya_ref[0], yb_ref[0]], axis=1),
                             wo_ref, g2_ref, wg_ref, wup_ref, wd_ref, store):
        stage()


def _attn_ffn_kernel(qT_ref, k_ref, vT_ref, km_ref, cols_ref,
                     x_ref, ya_ref, yb0_ref, wo_ref, g2_ref, wg_ref, wup_ref, wd_ref,
                     yb_ref, o_ref, kx_scr, yb_scr):
    r, p = pl.program_id(0), pl.program_id(1)
    n_pair = yb_scr.shape[1]

    @pl.when((r == 0) & (p == 0))
    def _():
        for q in range(n_pair):
            yb_scr[0, q] = yb0_ref[0, :, q * PAIR:(q + 1) * PAIR]

    rd = r % 2
    wr = 1 - rd

    def store_tile(jj, tile):
        yb_ref[0, jj * BLK:(jj + 1) * BLK, :] = tile
        yb_scr[wr, p, jj * BLK:(jj + 1) * BLK, :] = tile

    jobs, scores, softmax, pv = _attn_jobs(qT_ref, k_ref, vT_ref, km_ref, cols_ref, kx_scr, store_tile)

    subs = []
    for t in range(TM3 // SUB3):
        rows = slice(t * SUB3, (t + 1) * SUB3)

        def load_y(t=t, rows=rows):
            tok = pl.ds(pl.multiple_of(p * TM3 + t * SUB3, SUB3), SUB3)
            return jnp.concatenate(
                [ya_ref[0, rows, :]] + [yb_scr[rd, q, tok, :] for q in range(n_pair)], axis=1)

        def store(val, rows=rows):
            o_ref[0, rows, :] = val

        subs.append(_ffn_stages(lambda rows=rows: x_ref[0, rows, :], load_y,
                                wo_ref, g2_ref, wg_ref, wup_ref, wd_ref, store))
    (a_out, a_norm, a_gate, a_up, a_act, a_down), (b_out, b_norm, b_gate, b_up, b_act, b_down) = subs
    slots = [(a_out,), (b_out, a_norm), (a_gate, b_norm), (a_up, a_act),
             (b_gate,), (b_up, b_act), (a_down,), (b_down,)]
    per_slot = len(jobs) // len(slots)
    assert per_slot * len(slots) == len(jobs)
    for i, slot in enumerate(slots):
        mine = jobs[i * per_slot:(i + 1) * per_slot]
        started = [(*job, *scores(*job)) for job in mine]
        weighted = [softmax(*job) for job in started]
        for stage in slot:
            stage()
        for job in weighted:
            pv(*job)


def _full(shape):
    return pl.BlockSpec(shape, lambda *_: (0,) * len(shape), pipeline_mode=pl.Buffered(1))


def _layer(x, norm1_g, w_in, sgu_ln_g, sgu_ln_b, sgu_w, sgu_b, q_norm_g, k_norm_g,
           out_norm_a_g, out_norm_b_g, w_out, norm2_g, w_gate, w_up, w_down):
    B, S, D = x.shape
    nb = S // BLK
    assert D == D_MODEL and S % TM1 == 0

    w_in_t = w_in.astype(_BF16).T
    cols = [None] * N_COLS
    cols[COL_LN_G], cols[COL_LN_B] = sgu_ln_g, sgu_ln_b
    cols[COL_A_G], cols[COL_B_G] = out_norm_a_g, out_norm_b_g
    cols[COL_Q_G], cols[COL_K_G] = jnp.tile(q_norm_g, N_HEADS), jnp.tile(k_norm_g, N_HEADS)
    cols = jnp.stack(cols, axis=1).astype(_F32)

    steps1 = B * (S // TM1)
    ffn_w = (w_out, w_gate, w_up, w_down)
    assert all(w.shape[0] % (16 * steps1) == 0 for w in ffn_w)
    slab_specs = [pl.BlockSpec((w.shape[0] // steps1, w.shape[1]), lambda b, s: (b * (S // TM1) + s, 0))
                  for w in ffn_w]

    ya, qT, k, vT, km, wo_b, wg_b, wup_b, wd_b = pl.pallas_call(
        _inproj_kernel,
        grid=(B, S // TM1),
        in_specs=[
            pl.BlockSpec((1, TM1, D), lambda b, s: (b, s, 0)),
            _full((1, D)),
            _full(w_in_t.shape),
            _full((WIDTH, N_COLS)),
            _full((N_HEADS, CHUNK, CHUNK)), _full((N_HEADS, CHUNK)),
            *slab_specs,
        ],
        out_specs=[
            pl.BlockSpec((1, TM1, WIDTH), lambda b, s: (b, s, 0)),
            pl.BlockSpec((1, WIDTH, TM1), lambda b, s: (b, 0, s)),
            pl.BlockSpec((1, TM1, WIDTH), lambda b, s: (b, s, 0)),
            pl.BlockSpec((1, WIDTH, TM1), lambda b, s: (b, 0, s)),
            pl.BlockSpec((1, nb, WIDTH), lambda b, s: (b, 0, 0)),
            *slab_specs,
        ],
        out_shape=[
            jax.ShapeDtypeStruct((B, S, WIDTH), _BF16),
            jax.ShapeDtypeStruct((B, WIDTH, S), _BF16),
            jax.ShapeDtypeStruct((B, S, WIDTH), _BF16),
            jax.ShapeDtypeStruct((B, WIDTH, S), _BF16),
            jax.ShapeDtypeStruct((B, nb, WIDTH), _F32),
            *[jax.ShapeDtypeStruct(w.shape, _BF16) for w in ffn_w],
        ],
        compiler_params=pltpu.CompilerParams(
            dimension_semantics=("parallel", "arbitrary"),
            vmem_limit_bytes=VMEM_LIMIT_BYTES),
        name="inproj_sgu",
    )(x, norm1_g.reshape(1, D), w_in_t, cols, sgu_w, sgu_b, *ffn_w)

    n_pair = WIDTH // PAIR
    assert S == n_pair * TM3 and B >= 2

    def attn_specs(row):
        return [
            pl.BlockSpec((1, PAIR, S), lambda r, p: (row(r), p, 0)),
            pl.BlockSpec((1, S, PAIR), lambda r, p: (row(r), 0, p)),
            pl.BlockSpec((1, PAIR, S), lambda r, p: (row(r), p, 0)),
            pl.BlockSpec((1, nb, PAIR), lambda r, p: (row(r), 0, p)),
            pl.BlockSpec((PAIR, N_COLS), lambda r, p: (p, 0)),
        ]

    kx_scratch = pltpu.VMEM((2 * nb + S, PAIR), _BF16)
    ffn_w_specs = [_full((2 * WIDTH, D)), _full((1, D)), _full((D, D_FF)), _full((D, D_FF)), _full((D_FF, D))]
    ffn_w_args = (wo_b, norm2_g.reshape(1, D), wg_b, wup_b, wd_b)

    yb0 = pl.pallas_call(
        _attn_kernel,
        grid=(1, n_pair),
        in_specs=attn_specs(lambda r: r),
        out_specs=pl.BlockSpec((1, S, PAIR), lambda r, p: (r, 0, p)),
        out_shape=jax.ShapeDtypeStruct((1, S, WIDTH), _BF16),
        scratch_shapes=[kx_scratch],
        compiler_params=pltpu.CompilerParams(
            dimension_semantics=("parallel", "parallel"),
            vmem_limit_bytes=VMEM_LIMIT_BYTES),
        name="moba_attn",
    )(qT, k, vT, km, cols)

    yb, out = pl.pallas_call(
        _attn_ffn_kernel,
        grid=(B - 1, n_pair),
        in_specs=[
            *attn_specs(lambda r: r + 1),
            pl.BlockSpec((1, TM3, D), lambda r, p: (r, p, 0)),
            pl.BlockSpec((1, TM3, WIDTH), lambda r, p: (r, p, 0)),
            _full((1, S, WIDTH)),
            *ffn_w_specs,
        ],
        out_specs=[
            pl.BlockSpec((1, S, PAIR), lambda r, p: (r, 0, p)),
            pl.BlockSpec((1, TM3, D), lambda r, p: (r, p, 0)),
        ],
        out_shape=[
            jax.ShapeDtypeStruct((B - 1, S, WIDTH), _BF16),
            jax.ShapeDtypeStruct((B, S, D), _F32),
        ],
        scratch_shapes=[kx_scratch, pltpu.VMEM((2, n_pair, S, PAIR), _BF16)],
        compiler_params=pltpu.CompilerParams(
            dimension_semantics=("arbitrary", "arbitrary"),
            vmem_limit_bytes=VMEM_LIMIT_BYTES),
        name="attn_ffn",
    )(qT, k, vT, km, cols, x, ya, yb0, *ffn_w_args)

    last = B - 1
    out = pl.pallas_call(
        _ffn_kernel,
        grid=(n_pair,),
        in_specs=[
            pl.BlockSpec((1, TM3, D), lambda i: (last, i, 0)),
            pl.BlockSpec((1, TM3, WIDTH), lambda i: (last, i, 0)),
            pl.BlockSpec((1, TM3, WIDTH), lambda i: (last - 1, i, 0)),
            *ffn_w_specs,
            pl.BlockSpec(memory_space=pl.ANY),
        ],
        out_specs=pl.BlockSpec((1, TM3, D), lambda i: (last, i, 0)),
        out_shape=jax.ShapeDtypeStruct((B, S, D), _F32),
        input_output_aliases={8: 0},
        compiler_params=pltpu.CompilerParams(
            dimension_semantics=("parallel",),
            vmem_limit_bytes=VMEM_LIMIT_BYTES),
        name="outproj_ffn",
    )(x, ya, yb, *ffn_w_args, out)
    return out


def kernel(x, norm1_g, w_in, sgu_ln_g, sgu_ln_b, sgu_w, sgu_b, q_norm_g, k_norm_g,
           out_norm_a_g, out_norm_b_g, w_out, norm2_g, w_gate, w_up, w_down):
    depth = norm1_g.shape[0]
    for l in range(depth):
        x = _layer(x, norm1_g[l], w_in[l], sgu_ln_g[l], sgu_ln_b[l], sgu_w[l], sgu_b[l],
                   q_norm_g[l], k_norm_g[l], out_norm_a_g[l], out_norm_b_g[l], w_out[l],
                   norm2_g[l], w_gate[l], w_up[l], w_down[l])
    return x
```

```python
import jax
import jax.numpy as jnp
from jax import lax
from jax.experimental import pallas as pl
from jax.experimental.pallas import tpu as pltpu

D_MODEL = 1024
HEAD_DIM = 64
N_HEADS = 8
WIDTH = N_HEADS * HEAD_DIM
CHUNK = 128
BLK = 256
TOPK = 3
D_FF = 2816
EPS = 1e-6
Q_SCALE = HEAD_DIM ** -0.5 * 1.4426950408889634

TM1 = 1024
SUB1 = 256
MIX_HEADS = 2
TM3 = 512
SUB3 = 256
HEADS_PER_STEP = 2
PAIR = HEADS_PER_STEP * HEAD_DIM
SCORES_AHEAD = 2

COL_LN_G, COL_LN_B, COL_A_G, COL_B_G, COL_Q_G, COL_K_G = range(6)
N_COLS = 6

VMEM_LIMIT_BYTES = 56 * 1024 * 1024
_NT = (((1,), (1,)), ((), ()))
_F32 = jnp.float32
_BF16 = jnp.bfloat16


def _head_rms(t, gain):
    ms = jnp.mean(t * t, axis=0, keepdims=True)
    return t * lax.rsqrt(ms + EPS) * gain


def _inproj_kernel(x_ref, g1_ref, win_ref, cols_ref, sw_ref, sb_ref,
                   wo32_ref, wg32_ref, wup32_ref, wd32_ref,
                   ya_ref, qT_ref, k_ref, vT_ref, km_ref, wo_ref, wg_ref, wup_ref, wd_ref):
    for src, dst in ((wo32_ref, wo_ref), (wg32_ref, wg_ref), (wup32_ref, wup_ref), (wd32_ref, wd_ref)):
        dst[...] = src[...].astype(_BF16)

    si = pl.program_id(1)
    nsub = TM1 // SUB1
    row = lax.broadcasted_iota(jnp.int32, (CHUNK, CHUNK), 0)
    col = lax.broadcasted_iota(jnp.int32, (CHUNK, CHUNK), 1)
    causal = col <= row
    w_mix = [jnp.where(causal, sw_ref[hd], 0.0).astype(_BF16) for hd in range(N_HEADS)]
    w_mix = [jnp.concatenate(w_mix[h0:h0 + MIX_HEADS], axis=0)
             for h0 in range(0, N_HEADS, MIX_HEADS)]

    def project(t):
        x = x_ref[0, t * SUB1:(t + 1) * SUB1, :]
        ms = jnp.mean(x * x, axis=-1, keepdims=True)
        h = (x * lax.rsqrt(ms + EPS) * g1_ref[...]).astype(_BF16)
        return [lax.dot_general(win_ref[i * WIDTH:(i + 1) * WIDTH, :], h, _NT, preferred_element_type=_F32)
                for i in range(5)]

    pending = []
    for t in range(nsub):
        pending.append((t, project(t)))
        if len(pending) > 1:
            _inproj_finish(*pending.pop(0), si * nsub, w_mix, cols_ref, sb_ref,
                           ya_ref, qT_ref, k_ref, vT_ref, km_ref)
    _inproj_finish(*pending.pop(0), si * nsub, w_mix, cols_ref, sb_ref,
                   ya_ref, qT_ref, k_ref, vT_ref, km_ref)


def _inproj_finish(t, projs, blk0, w_mix, cols_ref, sb_ref, ya_ref, qT_ref, k_ref, vT_ref, km_ref):
    pu, pv, qt, kt, vbt = projs
    nchunk = SUB1 // CHUNK
    rows = slice(t * SUB1, (t + 1) * SUB1)

    u = jax.nn.gelu(pu)
    v = jax.nn.gelu(pv)
    ya = []
    per_head = nchunk * HEAD_DIM
    for h0 in range(0, N_HEADS, MIX_HEADS):
        heads = range(h0, h0 + MIX_HEADS)
        lhs = []
        for hd in heads:
            sl = slice(hd * HEAD_DIM, (hd + 1) * HEAD_DIM)
            vh = v[sl, :]
            mu = jnp.mean(vh, axis=0, keepdims=True)
            vc = vh - mu
            var = jnp.mean(vc * vc, axis=0, keepdims=True)
            vn = (vc * lax.rsqrt(var + EPS) * cols_ref[sl, COL_LN_G:COL_LN_G + 1]
                  + cols_ref[sl, COL_LN_B:COL_LN_B + 1])
            lhs += [vn[:, c * CHUNK:(c + 1) * CHUNK] for c in range(nchunk)]
        lhs = jnp.concatenate(lhs, axis=0).astype(_BF16)
        mixed_all = lax.dot_general(lhs, w_mix[h0 // MIX_HEADS], _NT, preferred_element_type=_F32)
        for i, hd in enumerate(heads):
            sl = slice(hd * HEAD_DIM, (hd + 1) * HEAD_DIM)
            mixed = mixed_all[i * per_head:(i + 1) * per_head, i * CHUNK:(i + 1) * CHUNK]
            mixed = mixed + sb_ref[hd:hd + 1, :]
            mixed = jnp.concatenate(
                [mixed[c * HEAD_DIM:(c + 1) * HEAD_DIM, :] for c in range(nchunk)], axis=1)
            ya.append(_head_rms(u[sl, :] * mixed, cols_ref[sl, COL_A_G:COL_A_G + 1]))
    ya_ref[0, rows, :] = jnp.concatenate(ya, axis=0).T.astype(_BF16)

    qn = jnp.concatenate(
        [_head_rms(qt[hd * HEAD_DIM:(hd + 1) * HEAD_DIM, :], cols_ref[0:HEAD_DIM, COL_Q_G:COL_Q_G + 1])
         for hd in range(N_HEADS)], axis=0) * Q_SCALE
    kn = jnp.concatenate(
        [_head_rms(kt[hd * HEAD_DIM:(hd + 1) * HEAD_DIM, :], cols_ref[0:HEAD_DIM, COL_K_G:COL_K_G + 1])
         for hd in range(N_HEADS)], axis=0)
    k_tok = kn.T
    qT_ref[0, :, rows] = qn.astype(_BF16)
    vT_ref[0, :, rows] = vbt.astype(_BF16)
    k_ref[0, rows, :] = k_tok.astype(_BF16)
    for b in range(SUB1 // BLK):
        kb = k_tok[b * BLK:(b + 1) * BLK, :]
        km_ref[0, pl.ds((blk0 + t) * (SUB1 // BLK) + b, 1), :] = jnp.mean(kb, axis=0, keepdims=True)


def _attn_jobs(qT_ref, k_ref, vT_ref, km_ref, cols_ref, kx_scr, store_tile):
    nb = k_ref.shape[1] // BLK
    frow = lax.broadcasted_iota(jnp.int32, (PAIR, BLK), 0)
    key_pos = lax.broadcasted_iota(jnp.int32, (BLK, BLK), 0)
    qry_pos = lax.broadcasted_iota(jnp.int32, (BLK, BLK), 1)
    causal_bias = jnp.where(key_pos <= qry_pos, 0.0, -jnp.inf)

    km = km_ref[0]
    km_hi = km.astype(_BF16)
    km_lo = (km - km_hi.astype(_F32)).astype(_BF16)
    kx_scr[0:2 * nb, :] = jnp.concatenate([km_hi, km_lo], axis=0)
    kx_scr[2 * nb:, :] = k_ref[0]

    def scores(jj, hh):
        nkeys = (jj + 1) * BLK
        qT = qT_ref[0, :, jj * BLK:(jj + 1) * BLK]
        in_head = (frow >= hh * HEAD_DIM) & (frow < (hh + 1) * HEAD_DIM)
        qm = jnp.where(in_head, qT, jnp.zeros_like(qT))
        bias = None
        if jj > TOPK:
            sx = jnp.dot(kx_scr[0:2 * nb + nkeys, :], qm, preferred_element_type=_F32)
            gate = sx[0:jj, :] + sx[nb:nb + jj, :]
            s = sx[2 * nb:, :]
            blk_id = lax.broadcasted_iota(jnp.int32, (jj, BLK), 0)
            rank = jnp.zeros((jj, BLK), _F32)
            for m in range(jj):
                gm = gate[m:m + 1, :]
                ahead = (gm > gate) | ((gm == gate) & (blk_id > m))
                rank = rank + jnp.where(ahead, 1.0, 0.0)
            bias = jnp.where(rank < TOPK, 0.0, -jnp.inf)
        else:
            s = jnp.dot(kx_scr[2 * nb:2 * nb + nkeys, :], qm, preferred_element_type=_F32)
        blocks, m = [], None
        for n in range(jj + 1):
            sn = s[n * BLK:(n + 1) * BLK, :]
            if n == jj:
                sn = sn + causal_bias
            elif bias is not None:
                sn = sn + bias[n:n + 1, :]
            blocks.append(sn)
            mn = jnp.max(sn, axis=0, keepdims=True)
            m = mn if m is None else jnp.maximum(m, mn)
        return blocks, m

    tiles = {}

    def softmax(jj, hh, blocks, m):
        probs, l = [], None
        for sn in blocks:
            p = jnp.exp2(sn - m)
            ln = jnp.sum(p, axis=0, keepdims=True)
            l = ln if l is None else l + ln
            probs.append(p.astype(_BF16))
        p = jnp.concatenate(probs, axis=0) if jj > 0 else probs[0]
        return jj, hh, p, l

    def pv(jj, hh, p, l):
        vrows = slice(hh * HEAD_DIM, (hh + 1) * HEAD_DIM)
        o = jnp.dot(vT_ref[0, vrows, 0:(jj + 1) * BLK], p, preferred_element_type=_F32)
        tiles.setdefault(jj, []).append(
            _head_rms(o / l, cols_ref[vrows, COL_B_G:COL_B_G + 1]))
        if len(tiles[jj]) == HEADS_PER_STEP:
            y = jnp.concatenate(tiles.pop(jj), axis=0)
            store_tile(jj, y.T.astype(_BF16))

    jobs = [(jj, hh) for jj in range(nb) for hh in range(HEADS_PER_STEP)]
    return jobs, scores, softmax, pv


def _ffn_stages(load_x, load_y, wo_ref, g2_ref, wg_ref, wup_ref, wd_ref, store):
    st = {}

    def outproj():
        st["x1"] = load_x() + jnp.dot(load_y(), wo_ref[...], preferred_element_type=_F32)

    def norm():
        x1 = st["x1"]
        ms = jnp.mean(x1 * x1, axis=-1, keepdims=True)
        st["h"] = (x1 * lax.rsqrt(ms + EPS) * g2_ref[...]).astype(_BF16)

    def gate():
        st["g"] = jnp.dot(st["h"], wg_ref[...], preferred_element_type=_F32)

    def up():
        st["u"] = jnp.dot(st.pop("h"), wup_ref[...], preferred_element_type=_F32)

    def act():
        st["a"] = (jax.nn.silu(st.pop("g")) * st.pop("u")).astype(_BF16)

    def down():
        store(st.pop("x1") + jnp.dot(st.pop("a"), wd_ref[...], preferred_element_type=_F32))

    return outproj, norm, gate, up, act, down


def _attn_ffn_kernel(qT_ref, k_ref, vT_ref, km_ref, cols_ref,
                     x_ref, ya_ref, wo_ref, g2_ref, wg_ref, wup_ref, wd_ref,
                     o_ref, kx_scr, yb_scr):
    r, p = pl.program_id(0), pl.program_id(1)
    last = pl.num_programs(0) - 1
    n_pair = yb_scr.shape[1]
    wr = r % 2
    rd = 1 - wr

    def attention():
        def store_tile(jj, tile):
            yb_scr[wr, p, jj * BLK:(jj + 1) * BLK, :] = tile
        return _attn_jobs(qT_ref, k_ref, vT_ref, km_ref, cols_ref, kx_scr, store_tile)

    def ffn_slots():
        subs = []
        for t in range(TM3 // SUB3):
            rows = slice(t * SUB3, (t + 1) * SUB3)

            def load_y(t=t, rows=rows):
                tok = pl.ds(pl.multiple_of(p * TM3 + t * SUB3, SUB3), SUB3)
                return jnp.concatenate(
                    [ya_ref[0, rows, :]] + [yb_scr[rd, q, tok, :] for q in range(n_pair)], axis=1)

            def store(val, rows=rows):
                o_ref[0, rows, :] = val

            subs.append(_ffn_stages(lambda rows=rows: x_ref[0, rows, :], load_y,
                                    wo_ref, g2_ref, wg_ref, wup_ref, wd_ref, store))
        (a_out, a_norm, a_gate, a_up, a_act, a_down), (b_out, b_norm, b_gate, b_up, b_act, b_down) = subs
        return [(a_out,), (b_out, a_norm), (a_gate, b_norm), (a_up, a_act),
                (b_gate,), (b_up, b_act), (a_down,), (b_down,)]

    @pl.when(r == 0)
    def _():
        jobs, scores, softmax, pv = attention()
        pending = []
        for job in jobs:
            pending.append((*job, *scores(*job)))
            if len(pending) > SCORES_AHEAD:
                pv(*softmax(*pending.pop(0)))
        for job in pending:
            pv(*softmax(*job))
        o_ref[...] = jnp.zeros(o_ref.shape, o_ref.dtype)

    @pl.when((r > 0) & (r < last))
    def _():
        jobs, scores, softmax, pv = attention()
        slots = ffn_slots()
        per_slot = len(jobs) // len(slots)
        assert per_slot * len(slots) == len(jobs)
        for i, slot in enumerate(slots):
            started = [(*job, *scores(*job)) for job in jobs[i * per_slot:(i + 1) * per_slot]]
            weighted = [softmax(*job) for job in started]
            for stage in slot:
                stage()
            for job in weighted:
                pv(*job)

    @pl.when(r == last)
    def _():
        for slot in ffn_slots():
            for stage in slot:
                stage()


def _full(shape):
    return pl.BlockSpec(shape, lambda *_: (0,) * len(shape), pipeline_mode=pl.Buffered(1))


def _layer(x, norm1_g, w_in, sgu_ln_g, sgu_ln_b, sgu_w, sgu_b, q_norm_g, k_norm_g,
           out_norm_a_g, out_norm_b_g, w_out, norm2_g, w_gate, w_up, w_down):
    B, S, D = x.shape
    nb = S // BLK
    assert D == D_MODEL and S % TM1 == 0

    w_in_t = w_in.astype(_BF16).T
    cols = [None] * N_COLS
    cols[COL_LN_G], cols[COL_LN_B] = sgu_ln_g, sgu_ln_b
    cols[COL_A_G], cols[COL_B_G] = out_norm_a_g, out_norm_b_g
    cols[COL_Q_G], cols[COL_K_G] = jnp.tile(q_norm_g, N_HEADS), jnp.tile(k_norm_g, N_HEADS)
    cols = jnp.stack(cols, axis=1).astype(_F32)

    steps1 = B * (S // TM1)
    ffn_w = (w_out, w_gate, w_up, w_down)
    assert all(w.shape[0] % (16 * steps1) == 0 for w in ffn_w)
    slab_specs = [pl.BlockSpec((w.shape[0] // steps1, w.shape[1]), lambda b, s: (b * (S // TM1) + s, 0))
                  for w in ffn_w]

    ya, qT, k, vT, km, wo_b, wg_b, wup_b, wd_b = pl.pallas_call(
        _inproj_kernel,
        grid=(B, S // TM1),
        in_specs=[
            pl.BlockSpec((1, TM1, D), lambda b, s: (b, s, 0)),
            _full((1, D)),
            _full(w_in_t.shape),
            _full((WIDTH, N_COLS)),
            _full((N_HEADS, CHUNK, CHUNK)), _full((N_HEADS, CHUNK)),
            *slab_specs,
        ],
        out_specs=[
            pl.BlockSpec((1, TM1, WIDTH), lambda b, s: (b, s, 0)),
            pl.BlockSpec((1, WIDTH, TM1), lambda b, s: (b, 0, s)),
            pl.BlockSpec((1, TM1, WIDTH), lambda b, s: (b, s, 0)),
            pl.BlockSpec((1, WIDTH, TM1), lambda b, s: (b, 0, s)),
            pl.BlockSpec((1, nb, WIDTH), lambda b, s: (b, 0, 0)),
            *slab_specs,
        ],
        out_shape=[
            jax.ShapeDtypeStruct((B, S, WIDTH), _BF16),
            jax.ShapeDtypeStruct((B, WIDTH, S), _BF16),
            jax.ShapeDtypeStruct((B, S, WIDTH), _BF16),
            jax.ShapeDtypeStruct((B, WIDTH, S), _BF16),
            jax.ShapeDtypeStruct((B, nb, WIDTH), _F32),
            *[jax.ShapeDtypeStruct(w.shape, _BF16) for w in ffn_w],
        ],
        compiler_params=pltpu.CompilerParams(
            dimension_semantics=("parallel", "arbitrary"),
            vmem_limit_bytes=VMEM_LIMIT_BYTES),
        name="inproj_sgu",
    )(x, norm1_g.reshape(1, D), w_in_t, cols, sgu_w, sgu_b, *ffn_w)

    n_pair = WIDTH // PAIR
    assert S == n_pair * TM3 and TM3 == 2 * SUB3
    att_row = lambda r: jnp.minimum(r, B - 1)
    ffn_row = lambda r: jnp.maximum(r - 1, 0)
    return pl.pallas_call(
        _attn_ffn_kernel,
        grid=(B + 1, n_pair),
        in_specs=[
            pl.BlockSpec((1, PAIR, S), lambda r, p: (att_row(r), p, 0)),
            pl.BlockSpec((1, S, PAIR), lambda r, p: (att_row(r), 0, p)),
            pl.BlockSpec((1, PAIR, S), lambda r, p: (att_row(r), p, 0)),
            pl.BlockSpec((1, nb, PAIR), lambda r, p: (att_row(r), 0, p)),
            pl.BlockSpec((PAIR, N_COLS), lambda r, p: (p, 0)),
            pl.BlockSpec((1, TM3, D), lambda r, p: (ffn_row(r), p, 0)),
            pl.BlockSpec((1, TM3, WIDTH), lambda r, p: (ffn_row(r), p, 0)),
            _full((2 * WIDTH, D)), _full((1, D)), _full((D, D_FF)), _full((D, D_FF)), _full((D_FF, D)),
        ],
        out_specs=pl.BlockSpec((1, TM3, D), lambda r, p: (ffn_row(r), p, 0)),
        out_shape=jax.ShapeDtypeStruct((B, S, D), _F32),
        scratch_shapes=[
            pltpu.VMEM((2 * nb + S, PAIR), _BF16),
            pltpu.VMEM((2, n_pair, S, PAIR), _BF16),
        ],
        compiler_params=pltpu.CompilerParams(
            dimension_semantics=("arbitrary", "arbitrary"),
            vmem_limit_bytes=VMEM_LIMIT_BYTES),
        name="attn_ffn",
    )(qT, k, vT, km, cols, x, ya, wo_b, norm2_g.reshape(1, D), wg_b, wup_b, wd_b)


def kernel(x, norm1_g, w_in, sgu_ln_g, sgu_ln_b, sgu_w, sgu_b, q_norm_g, k_norm_g,
           out_norm_a_g, out_norm_b_g, w_out, norm2_g, w_gate, w_up, w_down):
    depth = norm1_g.shape[0]
    for l in range(depth):
        x = _layer(x, norm1_g[l], w_in[l], sgu_ln_g[l], sgu_ln_b[l], sgu_w[l], sgu_b[l],
                   q_norm_g[l], k_norm_g[l], out_norm_a_g[l], out_norm_b_g[l], w_out[l],
                   norm2_g[l], w_gate[l], w_up[l], w_down[l])
    return x
```

```python
import jax
import jax.numpy as jnp
from jax import lax
from jax.experimental import pallas as pl
from jax.experimental.pallas import tpu as pltpu

D_MODEL = 1024
HEAD_DIM = 64
N_HEADS = 8
WIDTH = N_HEADS * HEAD_DIM
CHUNK = 128
BLK = 256
TOPK = 3
D_FF = 2816
EPS = 1e-6
Q_SCALE = HEAD_DIM ** -0.5 * 1.4426950408889634

TM1 = 1024
SUB1 = 256
MIX_HEADS = 2
TM3 = 512
SUB3 = 256
HEADS_PER_STEP = 2
PAIR = HEADS_PER_STEP * HEAD_DIM
SCORES_AHEAD = 2

COL_LN_G, COL_LN_B, COL_A_G, COL_B_G, COL_Q_G, COL_K_G = range(6)
N_COLS = 6

VMEM_LIMIT_BYTES = 56 * 1024 * 1024
_NT = (((1,), (1,)), ((), ()))
_F32 = jnp.float32
_BF16 = jnp.bfloat16


def _gelu_tanh(x):
    c = -2.0 * 0.7978845608028654 * 1.4426950408889634
    return x / (1.0 + jnp.exp2(x * (c + (c * 0.044715) * (x * x))))


def _head_rms(t, gain):
    ms = jnp.mean(t * t, axis=0, keepdims=True)
    return t * lax.rsqrt(ms + EPS) * gain


def _inproj_kernel(x_ref, g1_ref, win_ref, cols_ref, sw_ref, sb_ref,
                   wo32_ref, wg32_ref, wup32_ref, wd32_ref,
                   ya_ref, qT_ref, k_ref, vT_ref, km_ref, wo_ref, wg_ref, wup_ref, wd_ref):
    for src, dst in ((wo32_ref, wo_ref), (wg32_ref, wg_ref), (wup32_ref, wup_ref), (wd32_ref, wd_ref)):
        dst[...] = src[...].astype(_BF16)

    si = pl.program_id(1)
    nsub = TM1 // SUB1
    row = lax.broadcasted_iota(jnp.int32, (CHUNK, CHUNK), 0)
    col = lax.broadcasted_iota(jnp.int32, (CHUNK, CHUNK), 1)
    causal = col <= row
    w_mix = [jnp.where(causal, sw_ref[hd], 0.0).astype(_BF16) for hd in range(N_HEADS)]
    w_mix = [jnp.concatenate(w_mix[h0:h0 + MIX_HEADS], axis=0)
             for h0 in range(0, N_HEADS, MIX_HEADS)]

    def project(t):
        x = x_ref[0, t * SUB1:(t + 1) * SUB1, :]
        ms = jnp.mean(x * x, axis=-1, keepdims=True)
        h = (x * lax.rsqrt(ms + EPS) * g1_ref[...]).astype(_BF16)
        groups = []
        for lo, hi in ((0, 2), (2, 4), (4, 5)):
            pt = lax.dot_general(win_ref[lo * WIDTH:hi * WIDTH, :], h, _NT, preferred_element_type=_F32)
            groups += [pt[i * WIDTH:(i + 1) * WIDTH, :] for i in range(hi - lo)]
        return groups

    pending = []
    for t in range(nsub):
        pending.append((t, project(t)))
        if len(pending) > 1:
            _inproj_finish(*pending.pop(0), si * nsub, w_mix, cols_ref, sb_ref,
                           ya_ref, qT_ref, k_ref, vT_ref, km_ref)
    _inproj_finish(*pending.pop(0), si * nsub, w_mix, cols_ref, sb_ref,
                   ya_ref, qT_ref, k_ref, vT_ref, km_ref)


def _inproj_finish(t, projs, blk0, w_mix, cols_ref, sb_ref, ya_ref, qT_ref, k_ref, vT_ref, km_ref):
    pu, pv, qt, kt, vbt = projs
    nchunk = SUB1 // CHUNK
    rows = slice(t * SUB1, (t + 1) * SUB1)

    u = _gelu_tanh(pu)
    v = _gelu_tanh(pv)
    ya = []
    per_head = nchunk * HEAD_DIM
    for h0 in range(0, N_HEADS, MIX_HEADS):
        heads = range(h0, h0 + MIX_HEADS)
        lhs = []
        for hd in heads:
            sl = slice(hd * HEAD_DIM, (hd + 1) * HEAD_DIM)
            vh = v[sl, :]
            mu = jnp.mean(vh, axis=0, keepdims=True)
            vc = vh - mu
            var = jnp.mean(vc * vc, axis=0, keepdims=True)
            vn = (vc * lax.rsqrt(var + EPS) * cols_ref[sl, COL_LN_G:COL_LN_G + 1]
                  + cols_ref[sl, COL_LN_B:COL_LN_B + 1])
            lhs += [vn[:, c * CHUNK:(c + 1) * CHUNK] for c in range(nchunk)]
        lhs = jnp.concatenate(lhs, axis=0).astype(_BF16)
        mixed_all = lax.dot_general(lhs, w_mix[h0 // MIX_HEADS], _NT, preferred_element_type=_F32)
        for i, hd in enumerate(heads):
            sl = slice(hd * HEAD_DIM, (hd + 1) * HEAD_DIM)
            mixed = mixed_all[i * per_head:(i + 1) * per_head, i * CHUNK:(i + 1) * CHUNK]
            mixed = mixed + sb_ref[hd:hd + 1, :]
            mixed = jnp.concatenate(
                [mixed[c * HEAD_DIM:(c + 1) * HEAD_DIM, :] for c in range(nchunk)], axis=1)
            ya.append(_head_rms(u[sl, :] * mixed, cols_ref[sl, COL_A_G:COL_A_G + 1]))
    ya_ref[0, rows, :] = jnp.concatenate(ya, axis=0).T.astype(_BF16)

    q_gain = cols_ref[0:HEAD_DIM, COL_Q_G:COL_Q_G + 1] * Q_SCALE
    qn = jnp.concatenate(
        [_head_rms(qt[hd * HEAD_DIM:(hd + 1) * HEAD_DIM, :], q_gain) for hd in range(N_HEADS)], axis=0)
    kn = jnp.concatenate(
        [_head_rms(kt[hd * HEAD_DIM:(hd + 1) * HEAD_DIM, :], cols_ref[0:HEAD_DIM, COL_K_G:COL_K_G + 1])
         for hd in range(N_HEADS)], axis=0)
    k_tok = kn.T
    qT_ref[0, :, rows] = qn.astype(_BF16)
    vT_ref[0, :, rows] = vbt.astype(_BF16)
    k_ref[0, rows, :] = k_tok.astype(_BF16)
    for b in range(SUB1 // BLK):
        kb = k_tok[b * BLK:(b + 1) * BLK, :]
        km_ref[0, pl.ds((blk0 + t) * (SUB1 // BLK) + b, 1), :] = jnp.mean(kb, axis=0, keepdims=True)


def _attn_jobs(qT_ref, k_ref, vT_ref, km_ref, cols_ref, kx_scr, store_tile):
    nb = k_ref.shape[1] // BLK
    frow = lax.broadcasted_iota(jnp.int32, (PAIR, BLK), 0)
    key_pos = lax.broadcasted_iota(jnp.int32, (BLK, BLK), 0)
    qry_pos = lax.broadcasted_iota(jnp.int32, (BLK, BLK), 1)
    causal_bias = jnp.where(key_pos <= qry_pos, 0.0, -jnp.inf)

    km = km_ref[0]
    km_hi = km.astype(_BF16)
    km_lo = (km - km_hi.astype(_F32)).astype(_BF16)
    kx_scr[0:2 * nb, :] = jnp.concatenate([km_hi, km_lo], axis=0)
    kx_scr[2 * nb:, :] = k_ref[0]

    def scores(jj, hh):
        nkeys = (jj + 1) * BLK
        qT = qT_ref[0, :, jj * BLK:(jj + 1) * BLK]
        in_head = (frow >= hh * HEAD_DIM) & (frow < (hh + 1) * HEAD_DIM)
        qm = jnp.where(in_head, qT, jnp.zeros_like(qT))
        bias = None
        if jj > TOPK:
            sx = jnp.dot(kx_scr[0:2 * nb + nkeys, :], qm, preferred_element_type=_F32)
            gate = sx[0:jj, :] + sx[nb:nb + jj, :]
            s = sx[2 * nb:, :]
            blk_id = lax.broadcasted_iota(jnp.int32, (jj, BLK), 0)
            rank = jnp.zeros((jj, BLK), _F32)
            for m in range(jj):
                gm = gate[m:m + 1, :]
                ahead = (gm > gate) | ((gm == gate) & (blk_id > m))
                rank = rank + jnp.where(ahead, 1.0, 0.0)
            bias = jnp.where(rank < TOPK, 0.0, -jnp.inf)
        else:
            s = jnp.dot(kx_scr[2 * nb:2 * nb + nkeys, :], qm, preferred_element_type=_F32)
        blocks, m = [], None
        for n in range(jj + 1):
            sn = s[n * BLK:(n + 1) * BLK, :]
            if n == jj:
                sn = sn + causal_bias
            elif bias is not None:
                sn = sn + bias[n:n + 1, :]
            blocks.append(sn)
            mn = jnp.max(sn, axis=0, keepdims=True)
            m = mn if m is None else jnp.maximum(m, mn)
        return blocks, m

    tiles = {}

    def softmax(jj, hh, blocks, m):
        probs, l = [], None
        for sn in blocks:
            p = jnp.exp2(sn - m)
            ln = jnp.sum(p, axis=0, keepdims=True)
            l = ln if l is None else l + ln
            probs.append(p.astype(_BF16))
        p = jnp.concatenate(probs, axis=0) if jj > 0 else probs[0]
        return jj, hh, p, l

    def pv(jj, hh, p, l):
        vrows = slice(hh * HEAD_DIM, (hh + 1) * HEAD_DIM)
        o = jnp.dot(vT_ref[0, vrows, 0:(jj + 1) * BLK], p, preferred_element_type=_F32)
        tiles.setdefault(jj, []).append(
            _head_rms(o / l, cols_ref[vrows, COL_B_G:COL_B_G + 1]))
        if len(tiles[jj]) == HEADS_PER_STEP:
            y = jnp.concatenate(tiles.pop(jj), axis=0)
            store_tile(jj, y.T.astype(_BF16))

    jobs = [(jj, hh) for jj in range(nb) for hh in range(HEADS_PER_STEP)]
    return jobs, scores, softmax, pv


def _attn_kernel(qT_ref, k_ref, vT_ref, km_ref, cols_ref, yb_ref, kx_scr):
    def store_tile(jj, tile):
        yb_ref[0, jj * BLK:(jj + 1) * BLK, :] = tile

    jobs, scores, softmax, pv = _attn_jobs(qT_ref, k_ref, vT_ref, km_ref, cols_ref, kx_scr, store_tile)
    pending = []
    for job in jobs:
        pending.append((*job, *scores(*job)))
        if len(pending) > SCORES_AHEAD:
            pv(*softmax(*pending.pop(0)))
    for job in pending:
        pv(*softmax(*job))


def _ffn_stages(load_x, load_y, wo_ref, g2_ref, wg_ref, wup_ref, wd_ref, store):
    st = {}

    def outproj():
        st["x1"] = load_x() + jnp.dot(load_y(), wo_ref[...], preferred_element_type=_F32)

    def norm():
        x1 = st["x1"]
        ms = jnp.mean(x1 * x1, axis=-1, keepdims=True)
        st["h"] = (x1 * lax.rsqrt(ms + EPS) * g2_ref[...]).astype(_BF16)

    def gate():
        st["g"] = jnp.dot(st["h"], wg_ref[...], preferred_element_type=_F32)

    def up():
        st["u"] = jnp.dot(st.pop("h"), wup_ref[...], preferred_element_type=_F32)

    def act():
        st["a"] = (jax.nn.silu(st.pop("g")) * st.pop("u")).astype(_BF16)

    def down():
        store(st.pop("x1") + jnp.dot(st.pop("a"), wd_ref[...], preferred_element_type=_F32))

    return outproj, norm, gate, up, act, down


def _ffn_kernel(x_ref, ya_ref, yb_ref, wo_ref, g2_ref, wg_ref, wup_ref, wd_ref, _aliased_out, o_ref):
    def store(val):
        o_ref[0] = val

    for stage in _ffn_stages(lambda: x_ref[0],
                             lambda: jnp.concatenate([ya_ref[0], yb_ref[0]], axis=1),
                             wo_ref, g2_ref, wg_ref, wup_ref, wd_ref, store):
        stage()


def _attn_ffn_kernel(qT_ref, k_ref, vT_ref, km_ref, cols_ref,
                     x_ref, ya_ref, yb0_ref, wo_ref, g2_ref, wg_ref, wup_ref, wd_ref,
                     yb_ref, o_ref, kx_scr, yb_scr):
    r, p = pl.program_id(0), pl.program_id(1)
    n_pair = yb_scr.shape[1]

    @pl.when((r == 0) & (p == 0))
    def _():
        for q in range(n_pair):
            yb_scr[0, q] = yb0_ref[0, :, q * PAIR:(q + 1) * PAIR]

    rd = r % 2
    wr = 1 - rd

    def store_tile(jj, tile):
        yb_ref[0, jj * BLK:(jj + 1) * BLK, :] = tile
        yb_scr[wr, p, jj * BLK:(jj + 1) * BLK, :] = tile

    jobs, scores, softmax, pv = _attn_jobs(qT_ref, k_ref, vT_ref, km_ref, cols_ref, kx_scr, store_tile)

    subs = []
    for t in range(TM3 // SUB3):
        rows = slice(t * SUB3, (t + 1) * SUB3)

        def load_y(t=t, rows=rows):
            tok = pl.ds(pl.multiple_of(p * TM3 + t * SUB3, SUB3), SUB3)
            return jnp.concatenate(
                [ya_ref[0, rows, :]] + [yb_scr[rd, q, tok, :] for q in range(n_pair)], axis=1)

        def store(val, rows=rows):
            o_ref[0, rows, :] = val

        subs.append(_ffn_stages(lambda rows=rows: x_ref[0, rows, :], load_y,
                                wo_ref, g2_ref, wg_ref, wup_ref, wd_ref, store))
    (a_out, a_norm, a_gate, a_up, a_act, a_down), (b_out, b_norm, b_gate, b_up, b_act, b_down) = subs
    slots = [(a_out,), (b_out, a_norm), (a_gate, b_norm), (a_up, a_act),
             (b_gate,), (b_up, b_act), (a_down,), (b_down,)]
    per_slot = len(jobs) // len(slots)
    assert per_slot * len(slots) == len(jobs)
    for i, slot in enumerate(slots):
        mine = jobs[i * per_slot:(i + 1) * per_slot]
        started = [(*job, *scores(*job)) for job in mine]
        weighted = [softmax(*job) for job in started]
        for stage in slot:
            stage()
        for job in weighted:
            pv(*job)


def _full(shape):
    return pl.BlockSpec(shape, lambda *_: (0,) * len(shape), pipeline_mode=pl.Buffered(1))


def _layer(x, norm1_g, w_in, sgu_ln_g, sgu_ln_b, sgu_w, sgu_b, q_norm_g, k_norm_g,
           out_norm_a_g, out_norm_b_g, w_out, norm2_g, w_gate, w_up, w_down):
    B, S, D = x.shape
    nb = S // BLK
    assert D == D_MODEL and S % TM1 == 0

    w_in_t = w_in.astype(_BF16).T
    cols = [None] * N_COLS
    cols[COL_LN_G], cols[COL_LN_B] = sgu_ln_g, sgu_ln_b
    cols[COL_A_G], cols[COL_B_G] = out_norm_a_g, out_norm_b_g
    cols[COL_Q_G], cols[COL_K_G] = jnp.tile(q_norm_g, N_HEADS), jnp.tile(k_norm_g, N_HEADS)
    cols = jnp.stack(cols, axis=1).astype(_F32)

    steps1 = B * (S // TM1)
    ffn_w = (w_out, w_gate, w_up, w_down)
    assert all(w.shape[0] % (16 * steps1) == 0 for w in ffn_w)
    slab_specs = [pl.BlockSpec((w.shape[0] // steps1, w.shape[1]), lambda b, s: (b * (S // TM1) + s, 0))
                  for w in ffn_w]

    ya, qT, k, vT, km, wo_b, wg_b, wup_b, wd_b = pl.pallas_call(
        _inproj_kernel,
        grid=(B, S // TM1),
        in_specs=[
            pl.BlockSpec((1, TM1, D), lambda b, s: (b, s, 0)),
            _full((1, D)),
            _full(w_in_t.shape),
            _full((WIDTH, N_COLS)),
            _full((N_HEADS, CHUNK, CHUNK)), _full((N_HEADS, CHUNK)),
            *slab_specs,
        ],
        out_specs=[
            pl.BlockSpec((1, TM1, WIDTH), lambda b, s: (b, s, 0)),
            pl.BlockSpec((1, WIDTH, TM1), lambda b, s: (b, 0, s)),
            pl.BlockSpec((1, TM1, WIDTH), lambda b, s: (b, s, 0)),
            pl.BlockSpec((1, WIDTH, TM1), lambda b, s: (b, 0, s)),
            pl.BlockSpec((1, nb, WIDTH), lambda b, s: (b, 0, 0)),
            *slab_specs,
        ],
        out_shape=[
            jax.ShapeDtypeStruct((B, S, WIDTH), _BF16),
            jax.ShapeDtypeStruct((B, WIDTH, S), _BF16),
            jax.ShapeDtypeStruct((B, S, WIDTH), _BF16),
            jax.ShapeDtypeStruct((B, WIDTH, S), _BF16),
            jax.ShapeDtypeStruct((B, nb, WIDTH), _F32),
            *[jax.ShapeDtypeStruct(w.shape, _BF16) for w in ffn_w],
        ],
        compiler_params=pltpu.CompilerParams(
            dimension_semantics=("parallel", "arbitrary"),
            vmem_limit_bytes=VMEM_LIMIT_BYTES),
        name="inproj_sgu",
    )(x, norm1_g.reshape(1, D), w_in_t, cols, sgu_w, sgu_b, *ffn_w)

    n_pair = WIDTH // PAIR
    assert S == n_pair * TM3 and B >= 2

    def attn_specs(row):
        return [
            pl.BlockSpec((1, PAIR, S), lambda r, p: (row(r), p, 0)),
            pl.BlockSpec((1, S, PAIR), lambda r, p: (row(r), 0, p)),
            pl.BlockSpec((1, PAIR, S), lambda r, p: (row(r), p, 0)),
            pl.BlockSpec((1, nb, PAIR), lambda r, p: (row(r), 0, p)),
            pl.BlockSpec((PAIR, N_COLS), lambda r, p: (p, 0)),
        ]

    kx_scratch = pltpu.VMEM((2 * nb + S, PAIR), _BF16)
    ffn_w_specs = [_full((2 * WIDTH, D)), _full((1, D)), _full((D, D_FF)), _full((D, D_FF)), _full((D_FF, D))]
    ffn_w_args = (wo_b, norm2_g.reshape(1, D), wg_b, wup_b, wd_b)

    yb0 = pl.pallas_call(
        _attn_kernel,
        grid=(1, n_pair),
        in_specs=attn_specs(lambda r: r),
        out_specs=pl.BlockSpec((1, S, PAIR), lambda r, p: (r, 0, p)),
        out_shape=jax.ShapeDtypeStruct((1, S, WIDTH), _BF16),
        scratch_shapes=[kx_scratch],
        compiler_params=pltpu.CompilerParams(
            dimension_semantics=("parallel", "parallel"),
            vmem_limit_bytes=VMEM_LIMIT_BYTES),
        name="moba_attn",
    )(qT, k, vT, km, cols)

    yb, out = pl.pallas_call(
        _attn_ffn_kernel,
        grid=(B - 1, n_pair),
        in_specs=[
            *attn_specs(lambda r: r + 1),
            pl.BlockSpec((1, TM3, D), lambda r, p: (r, p, 0)),
            pl.BlockSpec((1, TM3, WIDTH), lambda r, p: (r, p, 0)),
            _full((1, S, WIDTH)),
            *ffn_w_specs,
        ],
        out_specs=[
            pl.BlockSpec((1, S, PAIR), lambda r, p: (r, 0, p)),
            pl.BlockSpec((1, TM3, D), lambda r, p: (r, p, 0)),
        ],
        out_shape=[
            jax.ShapeDtypeStruct((B - 1, S, WIDTH), _BF16),
            jax.ShapeDtypeStruct((B, S, D), _F32),
        ],
        scratch_shapes=[kx_scratch, pltpu.VMEM((2, n_pair, S, PAIR), _BF16)],
        compiler_params=pltpu.CompilerParams(
            dimension_semantics=("arbitrary", "arbitrary"),
            vmem_limit_bytes=VMEM_LIMIT_BYTES),
        name="attn_ffn",
    )(qT, k, vT, km, cols, x, ya, yb0, *ffn_w_args)

    last = B - 1
    out = pl.pallas_call(
        _ffn_kernel,
        grid=(n_pair,),
        in_specs=[
            pl.BlockSpec((1, TM3, D), lambda i: (last, i, 0)),
            pl.BlockSpec((1, TM3, WIDTH), lambda i: (last, i, 0)),
            pl.BlockSpec((1, TM3, WIDTH), lambda i: (last - 1, i, 0)),
            *ffn_w_specs,
            pl.BlockSpec(memory_space=pl.ANY),
        ],
        out_specs=pl.BlockSpec((1, TM3, D), lambda i: (last, i, 0)),
        out_shape=jax.ShapeDtypeStruct((B, S, D), _F32),
        input_output_aliases={8: 0},
        compiler_params=pltpu.CompilerParams(
            dimension_semantics=("parallel",),
            vmem_limit_bytes=VMEM_LIMIT_BYTES),
        name="outproj_ffn",
    )(x, ya, yb, *ffn_w_args, out)
    return out


def kernel(x, norm1_g, w_in, sgu_ln_g, sgu_ln_b, sgu_w, sgu_b, q_norm_g, k_norm_g,
           out_norm_a_g, out_norm_b_g, w_out, norm2_g, w_gate, w_up, w_down):
    depth = norm1_g.shape[0]
    for l in range(depth):
        x = _layer(x, norm1_g[l], w_in[l], sgu_ln_g[l], sgu_ln_b[l], sgu_w[l], sgu_b[l],
                   q_norm_g[l], k_norm_g[l], out_norm_a_g[l], out_norm_b_g[l], w_out[l],
                   norm2_g[l], w_gate[l], w_up[l], w_down[l])
    return x
```

```python
import jax
import jax.numpy as jnp
from jax import lax
from jax.experimental import pallas as pl
from jax.experimental.pallas import tpu as pltpu

D_MODEL = 1024
HEAD_DIM = 64
N_HEADS = 8
WIDTH = N_HEADS * HEAD_DIM
CHUNK = 128
BLK = 256
TOPK = 3
D_FF = 2816
EPS = 1e-6
Q_SCALE = HEAD_DIM ** -0.5 * 1.4426950408889634

TM1 = 1024
SUB1 = 256
MIX_HEADS = 2
TM3 = 512
SUB3 = 256
HEADS_PER_STEP = 2
PAIR = HEADS_PER_STEP * HEAD_DIM
SCORES_AHEAD = 2

COL_LN_G, COL_LN_B, COL_A_G, COL_B_G, COL_Q_G, COL_K_G = range(6)
N_COLS = 6

VMEM_LIMIT_BYTES = 56 * 1024 * 1024
KX_ROW_QUANTUM = 16 * 1024 // (PAIR * 2)
_NT = (((1,), (1,)), ((), ()))
_F32 = jnp.float32
_BF16 = jnp.bfloat16


def _head_rms(t, gain):
    ms = jnp.mean(t * t, axis=0, keepdims=True)
    return t * lax.rsqrt(ms + EPS) * gain


def _inproj_kernel(x_ref, g1_ref, win_ref, cols_ref, sw_ref, sb_ref,
                   wo32_ref, wg32_ref, wup32_ref, wd32_ref,
                   ya_ref, qT_ref, k_ref, vT_ref, km_ref, wo_ref, wg_ref, wup_ref, wd_ref):
    for src, dst in ((wo32_ref, wo_ref), (wg32_ref, wg_ref), (wup32_ref, wup_ref), (wd32_ref, wd_ref)):
        dst[...] = src[...].astype(_BF16)

    si = pl.program_id(1)
    nsub = TM1 // SUB1
    row = lax.broadcasted_iota(jnp.int32, (CHUNK, CHUNK), 0)
    col = lax.broadcasted_iota(jnp.int32, (CHUNK, CHUNK), 1)
    causal = col <= row
    w_mix = [jnp.where(causal, sw_ref[hd], 0.0).astype(_BF16) for hd in range(N_HEADS)]
    w_mix = [jnp.concatenate(w_mix[h0:h0 + MIX_HEADS], axis=0)
             for h0 in range(0, N_HEADS, MIX_HEADS)]

    def project(t):
        x = x_ref[0, t * SUB1:(t + 1) * SUB1, :]
        ms = jnp.mean(x * x, axis=-1, keepdims=True)
        h = (x * lax.rsqrt(ms + EPS) * g1_ref[...]).astype(_BF16)
        return [lax.dot_general(win_ref[i * WIDTH:(i + 1) * WIDTH, :], h, _NT, preferred_element_type=_F32)
                for i in range(5)]

    pending = []
    for t in range(nsub):
        pending.append((t, project(t)))
        if len(pending) > 1:
            _inproj_finish(*pending.pop(0), si * nsub, w_mix, cols_ref, sb_ref,
                           ya_ref, qT_ref, k_ref, vT_ref, km_ref)
    _inproj_finish(*pending.pop(0), si * nsub, w_mix, cols_ref, sb_ref,
                   ya_ref, qT_ref, k_ref, vT_ref, km_ref)


def _inproj_finish(t, projs, blk0, w_mix, cols_ref, sb_ref, ya_ref, qT_ref, k_ref, vT_ref, km_ref):
    pu, pv, qt, kt, vbt = projs
    nchunk = SUB1 // CHUNK
    rows = slice(t * SUB1, (t + 1) * SUB1)

    u = jax.nn.gelu(pu)
    v = jax.nn.gelu(pv)
    ya = []
    per_head = nchunk * HEAD_DIM
    for h0 in range(0, N_HEADS, MIX_HEADS):
        heads = range(h0, h0 + MIX_HEADS)
        lhs = []
        for hd in heads:
            sl = slice(hd * HEAD_DIM, (hd + 1) * HEAD_DIM)
            vh = v[sl, :]
            mu = jnp.mean(vh, axis=0, keepdims=True)
            vc = vh - mu
            var = jnp.mean(vc * vc, axis=0, keepdims=True)
            vn = (vc * lax.rsqrt(var + EPS) * cols_ref[sl, COL_LN_G:COL_LN_G + 1]
                  + cols_ref[sl, COL_LN_B:COL_LN_B + 1])
            lhs += [vn[:, c * CHUNK:(c + 1) * CHUNK] for c in range(nchunk)]
        lhs = jnp.concatenate(lhs, axis=0).astype(_BF16)
        mixed_all = lax.dot_general(lhs, w_mix[h0 // MIX_HEADS], _NT, preferred_element_type=_F32)
        for i, hd in enumerate(heads):
            sl = slice(hd * HEAD_DIM, (hd + 1) * HEAD_DIM)
            mixed = mixed_all[i * per_head:(i + 1) * per_head, i * CHUNK:(i + 1) * CHUNK]
            mixed = mixed + sb_ref[hd:hd + 1, :]
            mixed = jnp.concatenate(
                [mixed[c * HEAD_DIM:(c + 1) * HEAD_DIM, :] for c in range(nchunk)], axis=1)
            ya.append(_head_rms(u[sl, :] * mixed, cols_ref[sl, COL_A_G:COL_A_G + 1]))
    ya_ref[0, rows, :] = jnp.concatenate(ya, axis=0).T.astype(_BF16)

    qn = jnp.concatenate(
        [_head_rms(qt[hd * HEAD_DIM:(hd + 1) * HEAD_DIM, :], cols_ref[0:HEAD_DIM, COL_Q_G:COL_Q_G + 1])
         for hd in range(N_HEADS)], axis=0) * Q_SCALE
    kn = jnp.concatenate(
        [_head_rms(kt[hd * HEAD_DIM:(hd + 1) * HEAD_DIM, :], cols_ref[0:HEAD_DIM, COL_K_G:COL_K_G + 1])
         for hd in range(N_HEADS)], axis=0)
    k_tok = kn.T
    qT_ref[0, :, rows] = qn.astype(_BF16)
    vT_ref[0, :, rows] = vbt.astype(_BF16)
    k_ref[0, rows, :] = k_tok.astype(_BF16)
    for b in range(SUB1 // BLK):
        kb = k_tok[b * BLK:(b + 1) * BLK, :]
        km_ref[0, pl.ds((blk0 + t) * (SUB1 // BLK) + b, 1), :] = jnp.mean(kb, axis=0, keepdims=True)


def _attn_jobs(qT_ref, k_ref, vT_ref, km_ref, cols_ref, kx_scr, store_tile):
    nb = k_ref.shape[1] // BLK
    frow = lax.broadcasted_iota(jnp.int32, (PAIR, BLK), 0)
    key_pos = lax.broadcasted_iota(jnp.int32, (BLK, BLK), 0)
    qry_pos = lax.broadcasted_iota(jnp.int32, (BLK, BLK), 1)
    causal_bias = jnp.where(key_pos <= qry_pos, 0.0, -jnp.inf)

    km = km_ref[0]
    km_hi = km.astype(_BF16)
    km_lo = (km - km_hi.astype(_F32)).astype(_BF16)
    kx_scr[0:2 * nb, :] = jnp.concatenate([km_hi, km_lo], axis=0)
    kx_scr[2 * nb:2 * nb + nb * BLK, :] = k_ref[0]

    def scores(jj, hh):
        nkeys = (jj + 1) * BLK
        qT = qT_ref[0, :, jj * BLK:(jj + 1) * BLK]
        in_head = (frow >= hh * HEAD_DIM) & (frow < (hh + 1) * HEAD_DIM)
        qm = jnp.where(in_head, qT, jnp.zeros_like(qT))
        bias = None
        if jj > TOPK:
            sx = jnp.dot(kx_scr[0:2 * nb + nkeys, :], qm, preferred_element_type=_F32)
            gate = sx[0:jj, :] + sx[nb:nb + jj, :]
            s = sx[2 * nb:, :]
            blk_id = lax.broadcasted_iota(jnp.int32, (jj, BLK), 0)
            rank = jnp.zeros((jj, BLK), _F32)
            for m in range(jj):
                gm = gate[m:m + 1, :]
                ahead = (gm > gate) | ((gm == gate) & (blk_id > m))
                rank = rank + jnp.where(ahead, 1.0, 0.0)
            bias = jnp.where(rank < TOPK, 0.0, -jnp.inf)
        else:
            s = jnp.dot(kx_scr[2 * nb:2 * nb + nkeys, :], qm, preferred_element_type=_F32)
        blocks, m = [], None
        for n in range(jj + 1):
            sn = s[n * BLK:(n + 1) * BLK, :]
            if n == jj:
                sn = sn + causal_bias
            elif bias is not None:
                sn = sn + bias[n:n + 1, :]
            blocks.append(sn)
            mn = jnp.max(sn, axis=0, keepdims=True)
            m = mn if m is None else jnp.maximum(m, mn)
        return blocks, m

    tiles = {}

    def softmax(jj, hh, blocks, m):
        probs, l = [], None
        for sn in blocks:
            p = jnp.exp2(sn - m)
            ln = jnp.sum(p, axis=0, keepdims=True)
            l = ln if l is None else l + ln
            probs.append(p.astype(_BF16))
        p = jnp.concatenate(probs, axis=0) if jj > 0 else probs[0]
        return jj, hh, p, l

    def pv(jj, hh, p, l):
        vrows = slice(hh * HEAD_DIM, (hh + 1) * HEAD_DIM)
        o = jnp.dot(vT_ref[0, vrows, 0:(jj + 1) * BLK], p, preferred_element_type=_F32)
        tiles.setdefault(jj, []).append(
            _head_rms(o / l, cols_ref[vrows, COL_B_G:COL_B_G + 1]))
        if len(tiles[jj]) == HEADS_PER_STEP:
            y = jnp.concatenate(tiles.pop(jj), axis=0)
            store_tile(jj, y.T.astype(_BF16))

    jobs = [(jj, hh) for jj in range(nb) for hh in range(HEADS_PER_STEP)]
    return jobs, scores, softmax, pv


def _attn_kernel(qT_ref, k_ref, vT_ref, km_ref, cols_ref, yb_ref, kx_scr):
    def store_tile(jj, tile):
        yb_ref[0, jj * BLK:(jj + 1) * BLK, :] = tile

    jobs, scores, softmax, pv = _attn_jobs(qT_ref, k_ref, vT_ref, km_ref, cols_ref, kx_scr, store_tile)
    pending = []
    for job in jobs:
        pending.append((*job, *scores(*job)))
        if len(pending) > SCORES_AHEAD:
            pv(*softmax(*pending.pop(0)))
    for job in pending:
        pv(*softmax(*job))


def _ffn_stages(load_x, load_y, wo_ref, g2_ref, wg_ref, wup_ref, wd_ref, store):
    st = {}

    def outproj():
        st["x1"] = load_x() + jnp.dot(load_y(), wo_ref[...], preferred_element_type=_F32)

    def norm():
        x1 = st["x1"]
        ms = jnp.mean(x1 * x1, axis=-1, keepdims=True)
        st["h"] = (x1 * lax.rsqrt(ms + EPS) * g2_ref[...]).astype(_BF16)

    def gate():
        st["g"] = jnp.dot(st["h"], wg_ref[...], preferred_element_type=_F32)

    def up():
        st["u"] = jnp.dot(st.pop("h"), wup_ref[...], preferred_element_type=_F32)

    def act():
        st["a"] = (jax.nn.silu(st.pop("g")) * st.pop("u")).astype(_BF16)

    def down():
        store(st.pop("x1") + jnp.dot(st.pop("a"), wd_ref[...], preferred_element_type=_F32))

    return outproj, norm, gate, up, act, down


def _ffn_kernel(x_ref, ya_ref, yb_ref, wo_ref, g2_ref, wg_ref, wup_ref, wd_ref, _aliased_out, o_ref):
    def store(val):
        o_ref[0] = val

    for stage in _ffn_stages(lambda: x_ref[0],
                             lambda: jnp.concatenate([ya_ref[0], yb_ref[0]], axis=1),
                             wo_ref, g2_ref, wg_ref, wup_ref, wd_ref, store):
        stage()


def _attn_ffn_kernel(qT_ref, k_ref, vT_ref, km_ref, cols_ref,
                     x_ref, ya_ref, yb0_ref, wo_ref, g2_ref, wg_ref, wup_ref, wd_ref,
                     yb_ref, o_ref, kx_scr, yb_scr):
    r, p = pl.program_id(0), pl.program_id(1)
    n_pair = yb_scr.shape[1]

    @pl.when((r == 0) & (p == 0))
    def _():
        for q in range(n_pair):
            yb_scr[0, q] = yb0_ref[0, :, q * PAIR:(q + 1) * PAIR]

    rd = r % 2
    wr = 1 - rd

    def store_tile(jj, tile):
        yb_ref[0, jj * BLK:(jj + 1) * BLK, :] = tile
        yb_scr[wr, p, jj * BLK:(jj + 1) * BLK, :] = tile

    jobs, scores, softmax, pv = _attn_jobs(qT_ref, k_ref, vT_ref, km_ref, cols_ref, kx_scr, store_tile)

    subs = []
    for t in range(TM3 // SUB3):
        rows = slice(t * SUB3, (t + 1) * SUB3)

        def load_y(t=t, rows=rows):
            tok = pl.ds(pl.multiple_of(p * TM3 + t * SUB3, SUB3), SUB3)
            return jnp.concatenate(
                [ya_ref[0, rows, :]] + [yb_scr[rd, q, tok, :] for q in range(n_pair)], axis=1)

        def store(val, rows=rows):
            o_ref[0, rows, :] = val

        subs.append(_ffn_stages(lambda rows=rows: x_ref[0, rows, :], load_y,
                                wo_ref, g2_ref, wg_ref, wup_ref, wd_ref, store))
    (a_out, a_norm, a_gate, a_up, a_act, a_down), (b_out, b_norm, b_gate, b_up, b_act, b_down) = subs
    slots = [(a_out,), (b_out, a_norm), (a_gate, b_norm), (a_up, a_act),
             (b_gate,), (b_up, b_act), (a_down,), (b_down,)]
    per_slot = len(jobs) // len(slots)
    assert per_slot * len(slots) == len(jobs)
    for i, slot in enumerate(slots):
        mine = jobs[i * per_slot:(i + 1) * per_slot]
        started = [(*job, *scores(*job)) for job in mine]
        weighted = [softmax(*job) for job in started]
        for stage in slot:
            stage()
        for job in weighted:
            pv(*job)


def _full(shape):
    return pl.BlockSpec(shape, lambda *_: (0,) * len(shape), pipeline_mode=pl.Buffered(1))


def _layer(x, norm1_g, w_in, sgu_ln_g, sgu_ln_b, sgu_w, sgu_b, q_norm_g, k_norm_g,
           out_norm_a_g, out_norm_b_g, w_out, norm2_g, w_gate, w_up, w_down):
    B, S, D = x.shape
    nb = S // BLK
    assert D == D_MODEL and S % TM1 == 0

    w_in_t = w_in.astype(_BF16).T
    cols = [None] * N_COLS
    cols[COL_LN_G], cols[COL_LN_B] = sgu_ln_g, sgu_ln_b
    cols[COL_A_G], cols[COL_B_G] = out_norm_a_g, out_norm_b_g
    cols[COL_Q_G], cols[COL_K_G] = jnp.tile(q_norm_g, N_HEADS), jnp.tile(k_norm_g, N_HEADS)
    cols = jnp.stack(cols, axis=1).astype(_F32)

    steps1 = B * (S // TM1)
    ffn_w = (w_out, w_gate, w_up, w_down)
    assert all(w.shape[0] % (16 * steps1) == 0 for w in ffn_w)
    slab_specs = [pl.BlockSpec((w.shape[0] // steps1, w.shape[1]), lambda b, s: (b * (S // TM1) + s, 0))
                  for w in ffn_w]

    ya, qT, k, vT, km, wo_b, wg_b, wup_b, wd_b = pl.pallas_call(
        _inproj_kernel,
        grid=(B, S // TM1),
        in_specs=[
            pl.BlockSpec((1, TM1, D), lambda b, s: (b, s, 0)),
            _full((1, D)),
            _full(w_in_t.shape),
            _full((WIDTH, N_COLS)),
            _full((N_HEADS, CHUNK, CHUNK)), _full((N_HEADS, CHUNK)),
            *slab_specs,
        ],
        out_specs=[
            pl.BlockSpec((1, TM1, WIDTH), lambda b, s: (b, s, 0)),
            pl.BlockSpec((1, WIDTH, TM1), lambda b, s: (b, 0, s)),
            pl.BlockSpec((1, TM1, WIDTH), lambda b, s: (b, s, 0)),
            pl.BlockSpec((1, WIDTH, TM1), lambda b, s: (b, 0, s)),
            pl.BlockSpec((1, nb, WIDTH), lambda b, s: (b, 0, 0)),
            *slab_specs,
        ],
        out_shape=[
            jax.ShapeDtypeStruct((B, S, WIDTH), _BF16),
            jax.ShapeDtypeStruct((B, WIDTH, S), _BF16),
            jax.ShapeDtypeStruct((B, S, WIDTH), _BF16),
            jax.ShapeDtypeStruct((B, WIDTH, S), _BF16),
            jax.ShapeDtypeStruct((B, nb, WIDTH), _F32),
            *[jax.ShapeDtypeStruct(w.shape, _BF16) for w in ffn_w],
        ],
        compiler_params=pltpu.CompilerParams(
            dimension_semantics=("parallel", "arbitrary"),
            vmem_limit_bytes=VMEM_LIMIT_BYTES),
        name="inproj_sgu",
    )(x, norm1_g.reshape(1, D), w_in_t, cols, sgu_w, sgu_b, *ffn_w)

    n_pair = WIDTH // PAIR
    assert S == n_pair * TM3 and B >= 2

    def attn_specs(row):
        return [
            pl.BlockSpec((1, PAIR, S), lambda r, p: (row(r), p, 0)),
            pl.BlockSpec((1, S, PAIR), lambda r, p: (row(r), 0, p)),
            pl.BlockSpec((1, PAIR, S), lambda r, p: (row(r), p, 0)),
            pl.BlockSpec((1, nb, PAIR), lambda r, p: (row(r), 0, p)),
            pl.BlockSpec((PAIR, N_COLS), lambda r, p: (p, 0)),
        ]

    kx_rows = -(-(2 * nb + S) // KX_ROW_QUANTUM) * KX_ROW_QUANTUM
    kx_scratch = pltpu.VMEM((kx_rows, PAIR), _BF16)
    ffn_w_specs = [_full((2 * WIDTH, D)), _full((1, D)), _full((D, D_FF)), _full((D, D_FF)), _full((D_FF, D))]
    ffn_w_args = (wo_b, norm2_g.reshape(1, D), wg_b, wup_b, wd_b)

    yb0 = pl.pallas_call(
        _attn_kernel,
        grid=(1, n_pair),
        in_specs=attn_specs(lambda r: r),
        out_specs=pl.BlockSpec((1, S, PAIR), lambda r, p: (r, 0, p)),
        out_shape=jax.ShapeDtypeStruct((1, S, WIDTH), _BF16),
        scratch_shapes=[kx_scratch],
        compiler_params=pltpu.CompilerParams(
            dimension_semantics=("parallel", "parallel"),
            vmem_limit_bytes=VMEM_LIMIT_BYTES),
        name="moba_attn",
    )(qT, k, vT, km, cols)

    yb, out = pl.pallas_call(
        _attn_ffn_kernel,
        grid=(B - 1, n_pair),
        in_specs=[
            *attn_specs(lambda r: r + 1),
            pl.BlockSpec((1, TM3, D), lambda r, p: (r, p, 0)),
            pl.BlockSpec((1, TM3, WIDTH), lambda r, p: (r, p, 0)),
            _full((1, S, WIDTH)),
            *ffn_w_specs,
        ],
        out_specs=[
            pl.BlockSpec((1, S, PAIR), lambda r, p: (r, 0, p)),
            pl.BlockSpec((1, TM3, D), lambda r, p: (r, p, 0)),
        ],
        out_shape=[
            jax.ShapeDtypeStruct((B - 1, S, WIDTH), _BF16),
            jax.ShapeDtypeStruct((B, S, D), _F32),
        ],
        scratch_shapes=[kx_scratch, pltpu.VMEM((2, n_pair, S, PAIR), _BF16)],
        compiler_params=pltpu.CompilerParams(
            dimension_semantics=("arbitrary", "arbitrary"),
            vmem_limit_bytes=VMEM_LIMIT_BYTES),
        name="attn_ffn",
    )(qT, k, vT, km, cols, x, ya, yb0, *ffn_w_args)

    last = B - 1
    out = pl.pallas_call(
        _ffn_kernel,
        grid=(n_pair,),
        in_specs=[
            pl.BlockSpec((1, TM3, D), lambda i: (last, i, 0)),
            pl.BlockSpec((1, TM3, WIDTH), lambda i: (last, i, 0)),
            pl.BlockSpec((1, TM3, WIDTH), lambda i: (last - 1, i, 0)),
            *ffn_w_specs,
            pl.BlockSpec(memory_space=pl.ANY),
        ],
        out_specs=pl.BlockSpec((1, TM3, D), lambda i: (last, i, 0)),
        out_shape=jax.ShapeDtypeStruct((B, S, D), _F32),
        input_output_aliases={8: 0},
        compiler_params=pltpu.CompilerParams(
            dimension_semantics=("parallel",),
            vmem_limit_bytes=VMEM_LIMIT_BYTES),
        name="outproj_ffn",
    )(x, ya, yb, *ffn_w_args, out)
    return out


def kernel(x, norm1_g, w_in, sgu_ln_g, sgu_ln_b, sgu_w, sgu_b, q_norm_g, k_norm_g,
           out_norm_a_g, out_norm_b_g, w_out, norm2_g, w_gate, w_up, w_down):
    depth = norm1_g.shape[0]
    for l in range(depth):
        x = _layer(x, norm1_g[l], w_in[l], sgu_ln_g[l], sgu_ln_b[l], sgu_w[l], sgu_b[l],
                   q_norm_g[l], k_norm_g[l], out_norm_a_g[l], out_norm_b_g[l], w_out[l],
                   norm2_g[l], w_gate[l], w_up[l], w_down[l])
    return x
```

```python
import jax
import jax.numpy as jnp
from jax import lax
from jax.experimental import pallas as pl
from jax.experimental.pallas import tpu as pltpu

D_MODEL = 1024
HEAD_DIM = 64
N_HEADS = 8
WIDTH = N_HEADS * HEAD_DIM
CHUNK = 128
BLK = 256
TOPK = 3
D_FF = 2816
EPS = 1e-6
Q_SCALE = HEAD_DIM ** -0.5 * 1.4426950408889634

TM1 = 2048
SUB1 = 256
MIX_HEADS = 2
TM3 = 512
SUB3 = 256
HEADS_PER_STEP = 2
PAIR = HEADS_PER_STEP * HEAD_DIM
SCORES_AHEAD = 2

COL_LN_G, COL_LN_B, COL_A_G, COL_B_G, COL_Q_G, COL_K_G = range(6)
N_COLS = 6

VMEM_LIMIT_BYTES = 56 * 1024 * 1024
_NT = (((1,), (1,)), ((), ()))
_F32 = jnp.float32
_BF16 = jnp.bfloat16


def _head_rms(t, gain):
    ms = jnp.mean(t * t, axis=0, keepdims=True)
    return t * lax.rsqrt(ms + EPS) * gain


def _inproj_kernel(x_ref, g1_ref, win_ref, cols_ref, sw_ref, sb_ref,
                   ya_ref, qT_ref, k_ref, vT_ref, km_ref):
    si = pl.program_id(1)
    nsub = TM1 // SUB1
    row = lax.broadcasted_iota(jnp.int32, (CHUNK, CHUNK), 0)
    col = lax.broadcasted_iota(jnp.int32, (CHUNK, CHUNK), 1)
    causal = col <= row
    w_mix = [jnp.where(causal, sw_ref[hd], 0.0).astype(_BF16) for hd in range(N_HEADS)]
    w_mix = [jnp.concatenate(w_mix[h0:h0 + MIX_HEADS], axis=0)
             for h0 in range(0, N_HEADS, MIX_HEADS)]

    def project(t):
        x = x_ref[0, t * SUB1:(t + 1) * SUB1, :]
        ms = jnp.mean(x * x, axis=-1, keepdims=True)
        h = (x * lax.rsqrt(ms + EPS) * g1_ref[...]).astype(_BF16)
        return [lax.dot_general(win_ref[i * WIDTH:(i + 1) * WIDTH, :], h, _NT, preferred_element_type=_F32)
                for i in range(5)]

    pending = []
    for t in range(nsub):
        pending.append((t, project(t)))
        if len(pending) > 1:
            _inproj_finish(*pending.pop(0), si * nsub, w_mix, cols_ref, sb_ref,
                           ya_ref, qT_ref, k_ref, vT_ref, km_ref)
    _inproj_finish(*pending.pop(0), si * nsub, w_mix, cols_ref, sb_ref,
                   ya_ref, qT_ref, k_ref, vT_ref, km_ref)


def _inproj_finish(t, projs, blk0, w_mix, cols_ref, sb_ref, ya_ref, qT_ref, k_ref, vT_ref, km_ref):
    pu, pv, qt, kt, vbt = projs
    nchunk = SUB1 // CHUNK
    rows = slice(t * SUB1, (t + 1) * SUB1)

    u = jax.nn.gelu(pu)
    v = jax.nn.gelu(pv)
    ya = []
    per_head = nchunk * HEAD_DIM
    for h0 in range(0, N_HEADS, MIX_HEADS):
        heads = range(h0, h0 + MIX_HEADS)
        lhs = []
        for hd in heads:
            sl = slice(hd * HEAD_DIM, (hd + 1) * HEAD_DIM)
            vh = v[sl, :]
            mu = jnp.mean(vh, axis=0, keepdims=True)
            vc = vh - mu
            var = jnp.mean(vc * vc, axis=0, keepdims=True)
            vn = (vc * lax.rsqrt(var + EPS) * cols_ref[sl, COL_LN_G:COL_LN_G + 1]
                  + cols_ref[sl, COL_LN_B:COL_LN_B + 1])
            lhs += [vn[:, c * CHUNK:(c + 1) * CHUNK] for c in range(nchunk)]
        lhs = jnp.concatenate(lhs, axis=0).astype(_BF16)
        mixed_all = lax.dot_general(lhs, w_mix[h0 // MIX_HEADS], _NT, preferred_element_type=_F32)
        for i, hd in enumerate(heads):
            sl = slice(hd * HEAD_DIM, (hd + 1) * HEAD_DIM)
            mixed = mixed_all[i * per_head:(i + 1) * per_head, i * CHUNK:(i + 1) * CHUNK]
            mixed = mixed + sb_ref[hd:hd + 1, :]
            mixed = jnp.concatenate(
                [mixed[c * HEAD_DIM:(c + 1) * HEAD_DIM, :] for c in range(nchunk)], axis=1)
            ya.append(_head_rms(u[sl, :] * mixed, cols_ref[sl, COL_A_G:COL_A_G + 1]))
    ya_ref[0, rows, :] = jnp.concatenate(ya, axis=0).T.astype(_BF16)

    qn = jnp.concatenate(
        [_head_rms(qt[hd * HEAD_DIM:(hd + 1) * HEAD_DIM, :], cols_ref[0:HEAD_DIM, COL_Q_G:COL_Q_G + 1])
         for hd in range(N_HEADS)], axis=0) * Q_SCALE
    kn = jnp.concatenate(
        [_head_rms(kt[hd * HEAD_DIM:(hd + 1) * HEAD_DIM, :], cols_ref[0:HEAD_DIM, COL_K_G:COL_K_G + 1])
         for hd in range(N_HEADS)], axis=0)
    k_tok = kn.T
    qT_ref[0, :, rows] = qn.astype(_BF16)
    vT_ref[0, :, rows] = vbt.astype(_BF16)
    k_ref[0, rows, :] = k_tok.astype(_BF16)
    for b in range(SUB1 // BLK):
        kb = k_tok[b * BLK:(b + 1) * BLK, :]
        km_ref[0, pl.ds((blk0 + t) * (SUB1 // BLK) + b, 1), :] = jnp.mean(kb, axis=0, keepdims=True)


def _attn_jobs(qT_ref, k_ref, vT_ref, km_ref, cols_ref, kx_scr, store_tile):
    nb = k_ref.shape[1] // BLK
    frow = lax.broadcasted_iota(jnp.int32, (PAIR, BLK), 0)
    key_pos = lax.broadcasted_iota(jnp.int32, (BLK, BLK), 0)
    qry_pos = lax.broadcasted_iota(jnp.int32, (BLK, BLK), 1)
    causal_bias = jnp.where(key_pos <= qry_pos, 0.0, -jnp.inf)

    km = km_ref[0]
    km_hi = km.astype(_BF16)
    km_lo = (km - km_hi.astype(_F32)).astype(_BF16)
    kx_scr[0:2 * nb, :] = jnp.concatenate([km_hi, km_lo], axis=0)
    kx_scr[2 * nb:, :] = k_ref[0]

    def scores(jj, hh):
        nkeys = (jj + 1) * BLK
        qT = qT_ref[0, :, jj * BLK:(jj + 1) * BLK]
        in_head = (frow >= hh * HEAD_DIM) & (frow < (hh + 1) * HEAD_DIM)
        qm = jnp.where(in_head, qT, jnp.zeros_like(qT))
        bias = None
        if jj > TOPK:
            sx = jnp.dot(kx_scr[0:2 * nb + nkeys, :], qm, preferred_element_type=_F32)
            gate = sx[0:jj, :] + sx[nb:nb + jj, :]
            s = sx[2 * nb:, :]
            blk_id = lax.broadcasted_iota(jnp.int32, (jj, BLK), 0)
            rank = jnp.zeros((jj, BLK), _F32)
            for m in range(jj):
                gm = gate[m:m + 1, :]
                ahead = (gm > gate) | ((gm == gate) & (blk_id > m))
                rank = rank + jnp.where(ahead, 1.0, 0.0)
            bias = jnp.where(rank < TOPK, 0.0, -jnp.inf)
        else:
            s = jnp.dot(kx_scr[2 * nb:2 * nb + nkeys, :], qm, preferred_element_type=_F32)
        blocks, m = [], None
        for n in range(jj + 1):
            sn = s[n * BLK:(n + 1) * BLK, :]
            if n == jj:
                sn = sn + causal_bias
            elif bias is not None:
                sn = sn + bias[n:n + 1, :]
            blocks.append(sn)
            mn = jnp.max(sn, axis=0, keepdims=True)
            m = mn if m is None else jnp.maximum(m, mn)
        return blocks, m

    tiles = {}

    def softmax(jj, hh, blocks, m):
        probs, l = [], None
        for sn in blocks:
            p = jnp.exp2(sn - m)
            ln = jnp.sum(p, axis=0, keepdims=True)
            l = ln if l is None else l + ln
            probs.append(p.astype(_BF16))
        p = jnp.concatenate(probs, axis=0) if jj > 0 else probs[0]
        return jj, hh, p, l

    def pv(jj, hh, p, l):
        vrows = slice(hh * HEAD_DIM, (hh + 1) * HEAD_DIM)
        o = jnp.dot(vT_ref[0, vrows, 0:(jj + 1) * BLK], p, preferred_element_type=_F32)
        tiles.setdefault(jj, []).append(
            _head_rms(o / l, cols_ref[vrows, COL_B_G:COL_B_G + 1]))
        if len(tiles[jj]) == HEADS_PER_STEP:
            y = jnp.concatenate(tiles.pop(jj), axis=0)
            store_tile(jj, y.T.astype(_BF16))

    jobs = [(jj, hh) for jj in range(nb) for hh in range(HEADS_PER_STEP)]
    return jobs, scores, softmax, pv


def _attn_kernel(qT_ref, k_ref, vT_ref, km_ref, cols_ref, wo32_ref, wg32_ref, wup32_ref, wd32_ref,
                 yb_ref, wo_ref, wg_ref, wup_ref, wd_ref, kx_scr):
    for src, dst in ((wo32_ref, wo_ref), (wg32_ref, wg_ref), (wup32_ref, wup_ref), (wd32_ref, wd_ref)):
        dst[...] = src[...].astype(_BF16)

    def store_tile(jj, tile):
        yb_ref[0, jj * BLK:(jj + 1) * BLK, :] = tile

    jobs, scores, softmax, pv = _attn_jobs(qT_ref, k_ref, vT_ref, km_ref, cols_ref, kx_scr, store_tile)
    pending = []
    for job in jobs:
        pending.append((*job, *scores(*job)))
        if len(pending) > SCORES_AHEAD:
            pv(*softmax(*pending.pop(0)))
    for job in pending:
        pv(*softmax(*job))


def _ffn_stages(load_x, load_y, wo_ref, g2_ref, wg_ref, wup_ref, wd_ref, store):
    st = {}

    def outproj():
        st["x1"] = load_x() + jnp.dot(load_y(), wo_ref[...], preferred_element_type=_F32)

    def norm():
        x1 = st["x1"]
        ms = jnp.mean(x1 * x1, axis=-1, keepdims=True)
        st["h"] = (x1 * lax.rsqrt(ms + EPS) * g2_ref[...]).astype(_BF16)

    def gate():
        st["g"] = jnp.dot(st["h"], wg_ref[...], preferred_element_type=_F32)

    def up():
        st["u"] = jnp.dot(st.pop("h"), wup_ref[...], preferred_element_type=_F32)

    def act():
        st["a"] = (jax.nn.silu(st.pop("g")) * st.pop("u")).astype(_BF16)

    def down():
        store(st.pop("x1") + jnp.dot(st.pop("a"), wd_ref[...], preferred_element_type=_F32))

    return outproj, norm, gate, up, act, down


def _ffn_kernel(x_ref, ya_ref, yb_ref, wo_ref, g2_ref, wg_ref, wup_ref, wd_ref, _aliased_out, o_ref):
    def store(val):
        o_ref[0] = val

    for stage in _ffn_stages(lambda: x_ref[0],
                             lambda: jnp.concatenate([ya_ref[0], yb_ref[0]], axis=1),
                             wo_ref, g2_ref, wg_ref, wup_ref, wd_ref, store):
        stage()


def _attn_ffn_kernel(qT_ref, k_ref, vT_ref, km_ref, cols_ref,
                     x_ref, ya_ref, yb0_ref, wo_ref, g2_ref, wg_ref, wup_ref, wd_ref,
                     yb_ref, o_ref, kx_scr, yb_scr):
    r, p = pl.program_id(0), pl.program_id(1)
    n_pair = yb_scr.shape[1]

    @pl.when((r == 0) & (p == 0))
    def _():
        for q in range(n_pair):
            yb_scr[0, q] = yb0_ref[0, :, q * PAIR:(q + 1) * PAIR]

    rd = r % 2
    wr = 1 - rd

    def store_tile(jj, tile):
        yb_ref[0, jj * BLK:(jj + 1) * BLK, :] = tile
        yb_scr[wr, p, jj * BLK:(jj + 1) * BLK, :] = tile

    jobs, scores, softmax, pv = _attn_jobs(qT_ref, k_ref, vT_ref, km_ref, cols_ref, kx_scr, store_tile)

    subs = []
    for t in range(TM3 // SUB3):
        rows = slice(t * SUB3, (t + 1) * SUB3)

        def load_y(t=t, rows=rows):
            tok = pl.ds(pl.multiple_of(p * TM3 + t * SUB3, SUB3), SUB3)
            return jnp.concatenate(
                [ya_ref[0, rows, :]] + [yb_scr[rd, q, tok, :] for q in range(n_pair)], axis=1)

        def store(val, rows=rows):
            o_ref[0, rows, :] = val

        subs.append(_ffn_stages(lambda rows=rows: x_ref[0, rows, :], load_y,
                                wo_ref, g2_ref, wg_ref, wup_ref, wd_ref, store))
    (a_out, a_norm, a_gate, a_up, a_act, a_down), (b_out, b_norm, b_gate, b_up, b_act, b_down) = subs
    slots = [(a_out,), (b_out, a_norm), (a_gate, b_norm), (a_up, a_act),
             (b_gate,), (b_up, b_act), (a_down,), (b_down,)]
    per_slot = len(jobs) // len(slots)
    assert per_slot * len(slots) == len(jobs)
    for i, slot in enumerate(slots):
        mine = jobs[i * per_slot:(i + 1) * per_slot]
        started = [(*job, *scores(*job)) for job in mine]
        weighted = [softmax(*job) for job in started]
        for stage in slot:
            stage()
        for job in weighted:
            pv(*job)


def _full(shape):
    return pl.BlockSpec(shape, lambda *_: (0,) * len(shape), pipeline_mode=pl.Buffered(1))


def _layer(x, norm1_g, w_in, sgu_ln_g, sgu_ln_b, sgu_w, sgu_b, q_norm_g, k_norm_g,
           out_norm_a_g, out_norm_b_g, w_out, norm2_g, w_gate, w_up, w_down):
    B, S, D = x.shape
    nb = S // BLK
    assert D == D_MODEL and S % TM1 == 0

    w_in_t = w_in.astype(_BF16).T
    cols = [None] * N_COLS
    cols[COL_LN_G], cols[COL_LN_B] = sgu_ln_g, sgu_ln_b
    cols[COL_A_G], cols[COL_B_G] = out_norm_a_g, out_norm_b_g
    cols[COL_Q_G], cols[COL_K_G] = jnp.tile(q_norm_g, N_HEADS), jnp.tile(k_norm_g, N_HEADS)
    cols = jnp.stack(cols, axis=1).astype(_F32)

    ya, qT, k, vT, km = pl.pallas_call(
        _inproj_kernel,
        grid=(B, S // TM1),
        in_specs=[
            pl.BlockSpec((1, TM1, D), lambda b, s: (b, s, 0)),
            _full((1, D)),
            _full(w_in_t.shape),
            _full((WIDTH, N_COLS)),
            _full((N_HEADS, CHUNK, CHUNK)), _full((N_HEADS, CHUNK)),
        ],
        out_specs=[
            pl.BlockSpec((1, TM1, WIDTH), lambda b, s: (b, s, 0)),
            pl.BlockSpec((1, WIDTH, TM1), lambda b, s: (b, 0, s)),
            pl.BlockSpec((1, TM1, WIDTH), lambda b, s: (b, s, 0)),
            pl.BlockSpec((1, WIDTH, TM1), lambda b, s: (b, 0, s)),
            pl.BlockSpec((1, nb, WIDTH), lambda b, s: (b, 0, 0)),
        ],
        out_shape=[
            jax.ShapeDtypeStruct((B, S, WIDTH), _BF16),
            jax.ShapeDtypeStruct((B, WIDTH, S), _BF16),
            jax.ShapeDtypeStruct((B, S, WIDTH), _BF16),
            jax.ShapeDtypeStruct((B, WIDTH, S), _BF16),
            jax.ShapeDtypeStruct((B, nb, WIDTH), _F32),
        ],
        compiler_params=pltpu.CompilerParams(
            dimension_semantics=("parallel", "arbitrary"),
            vmem_limit_bytes=VMEM_LIMIT_BYTES),
        name="inproj_sgu",
    )(x, norm1_g.reshape(1, D), w_in_t, cols, sgu_w, sgu_b)

    n_pair = WIDTH // PAIR
    assert S == n_pair * TM3 and B >= 2

    def attn_specs(row):
        return [
            pl.BlockSpec((1, PAIR, S), lambda r, p: (row(r), p, 0)),
            pl.BlockSpec((1, S, PAIR), lambda r, p: (row(r), 0, p)),
            pl.BlockSpec((1, PAIR, S), lambda r, p: (row(r), p, 0)),
            pl.BlockSpec((1, nb, PAIR), lambda r, p: (row(r), 0, p)),
            pl.BlockSpec((PAIR, N_COLS), lambda r, p: (p, 0)),
        ]

    kx_scratch = pltpu.VMEM((2 * nb + S, PAIR), _BF16)
    ffn_w_specs = [_full((2 * WIDTH, D)), _full((1, D)), _full((D, D_FF)), _full((D, D_FF)), _full((D_FF, D))]

    ffn_w = (w_out, w_gate, w_up, w_down)
    assert all(w.shape[0] % (16 * n_pair) == 0 for w in ffn_w)
    slab_specs = [pl.BlockSpec((w.shape[0] // n_pair, w.shape[1]), lambda r, p: (p, 0)) for w in ffn_w]
    yb0, wo_b, wg_b, wup_b, wd_b = pl.pallas_call(
        _attn_kernel,
        grid=(1, n_pair),
        in_specs=[*attn_specs(lambda r: r), *slab_specs],
        out_specs=[pl.BlockSpec((1, S, PAIR), lambda r, p: (r, 0, p)), *slab_specs],
        out_shape=[jax.ShapeDtypeStruct((1, S, WIDTH), _BF16),
                   *[jax.ShapeDtypeStruct(w.shape, _BF16) for w in ffn_w]],
        scratch_shapes=[kx_scratch],
        compiler_params=pltpu.CompilerParams(
            dimension_semantics=("parallel", "parallel"),
            vmem_limit_bytes=VMEM_LIMIT_BYTES),
        name="moba_attn",
    )(qT, k, vT, km, cols, *ffn_w)
    ffn_w_args = (wo_b, norm2_g.reshape(1, D), wg_b, wup_b, wd_b)

    yb, out = pl.pallas_call(
        _attn_ffn_kernel,
        grid=(B - 1, n_pair),
        in_specs=[
            *attn_specs(lambda r: r + 1),
            pl.BlockSpec((1, TM3, D), lambda r, p: (r, p, 0)),
            pl.BlockSpec((1, TM3, WIDTH), lambda r, p: (r, p, 0)),
            _full((1, S, WIDTH)),
            *ffn_w_specs,
        ],
        out_specs=[
            pl.BlockSpec((1, S, PAIR), lambda r, p: (r, 0, p)),
            pl.BlockSpec((1, TM3, D), lambda r, p: (r, p, 0)),
        ],
        out_shape=[
            jax.ShapeDtypeStruct((B - 1, S, WIDTH), _BF16),
            jax.ShapeDtypeStruct((B, S, D), _F32),
        ],
        scratch_shapes=[kx_scratch, pltpu.VMEM((2, n_pair, S, PAIR), _BF16)],
        compiler_params=pltpu.CompilerParams(
            dimension_semantics=("arbitrary", "arbitrary"),
            vmem_limit_bytes=VMEM_LIMIT_BYTES),
        name="attn_ffn",
    )(qT, k, vT, km, cols, x, ya, yb0, *ffn_w_args)

    last = B - 1
    out = pl.pallas_call(
        _ffn_kernel,
        grid=(n_pair,),
        in_specs=[
            pl.BlockSpec((1, TM3, D), lambda i: (last, i, 0)),
            pl.BlockSpec((1, TM3, WIDTH), lambda i: (last, i, 0)),
            pl.BlockSpec((1, TM3, WIDTH), lambda i: (last - 1, i, 0)),
            *ffn_w_specs,
            pl.BlockSpec(memory_space=pl.ANY),
        ],
        out_specs=pl.BlockSpec((1, TM3, D), lambda i: (last, i, 0)),
        out_shape=jax.ShapeDtypeStruct((B, S, D), _F32),
        input_output_aliases={8: 0},
        compiler_params=pltpu.CompilerParams(
            dimension_semantics=("parallel",),
            vmem_limit_bytes=VMEM_LIMIT_BYTES),
        name="outproj_ffn",
    )(x, ya, yb, *ffn_w_args, out)
    return out


def kernel(x, norm1_g, w_in, sgu_ln_g, sgu_ln_b, sgu_w, sgu_b, q_norm_g, k_norm_g,
           out_norm_a_g, out_norm_b_g, w_out, norm2_g, w_gate, w_up, w_down):
    depth = norm1_g.shape[0]
    for l in range(depth):
        x = _layer(x, norm1_g[l], w_in[l], sgu_ln_g[l], sgu_ln_b[l], sgu_w[l], sgu_b[l],
                   q_norm_g[l], k_norm_g[l], out_norm_a_g[l], out_norm_b_g[l], w_out[l],
                   norm2_g[l], w_gate[l], w_up[l], w_down[l])
    return x
```

```python
import jax
import jax.numpy as jnp
from jax import lax
from jax.experimental import pallas as pl
from jax.experimental.pallas import tpu as pltpu

D_MODEL = 1024
HEAD_DIM = 64
N_HEADS = 8
WIDTH = N_HEADS * HEAD_DIM
CHUNK = 128
BLK = 256
TOPK = 3
D_FF = 2816
EPS = 1e-6
Q_SCALE = HEAD_DIM ** -0.5 * 1.4426950408889634

TM1 = 1024
SUB1 = 256
MIX_HEADS = 2
TM3 = 512
SUB3 = 256
HEADS_PER_STEP = 2
PAIR = HEADS_PER_STEP * HEAD_DIM
SCORES_AHEAD = 2

COL_LN_G, COL_LN_B, COL_A_G, COL_B_G, COL_Q_G, COL_K_G = range(6)
N_COLS = 6

VMEM_LIMIT_BYTES = 56 * 1024 * 1024
_NT = (((1,), (1,)), ((), ()))
_F32 = jnp.float32
_BF16 = jnp.bfloat16


def _head_rms(t, gain):
    ms = jnp.mean(t * t, axis=0, keepdims=True)
    return t * lax.rsqrt(ms + EPS) * gain


def _inproj_kernel(x_ref, g1_ref, win_ref, cols_ref, sw_ref, sb_ref,
                   wo32_ref, wg32_ref, wup32_ref, wd32_ref,
                   ya_ref, qT_ref, k_ref, vT_ref, km_ref, wo_ref, wg_ref, wup_ref, wd_ref):
    for src, dst in ((wo32_ref, wo_ref), (wg32_ref, wg_ref), (wup32_ref, wup_ref), (wd32_ref, wd_ref)):
        dst[...] = src[...].astype(_BF16)

    si = pl.program_id(1)
    nsub = TM1 // SUB1
    row = lax.broadcasted_iota(jnp.int32, (CHUNK, CHUNK), 0)
    col = lax.broadcasted_iota(jnp.int32, (CHUNK, CHUNK), 1)
    causal = col <= row
    w_mix = [jnp.where(causal, sw_ref[hd], 0.0).astype(_BF16) for hd in range(N_HEADS)]
    w_mix = [jnp.concatenate(w_mix[h0:h0 + MIX_HEADS], axis=0)
             for h0 in range(0, N_HEADS, MIX_HEADS)]

    def project(t):
        x = x_ref[0, t * SUB1:(t + 1) * SUB1, :]
        ms = jnp.mean(x * x, axis=-1, keepdims=True)
        h = (x * lax.rsqrt(ms + EPS) * g1_ref[...]).astype(_BF16)
        return [lax.dot_general(win_ref[i * WIDTH:(i + 1) * WIDTH, :], h, _NT, preferred_element_type=_F32)
                for i in range(5)]

    pending = []
    for t in range(nsub):
        pending.append((t, project(t)))
        if len(pending) > 1:
            _inproj_finish(*pending.pop(0), si * nsub, w_mix, cols_ref, sb_ref,
                           ya_ref, qT_ref, k_ref, vT_ref, km_ref)
    _inproj_finish(*pending.pop(0), si * nsub, w_mix, cols_ref, sb_ref,
                   ya_ref, qT_ref, k_ref, vT_ref, km_ref)


def _inproj_finish(t, projs, blk0, w_mix, cols_ref, sb_ref, ya_ref, qT_ref, k_ref, vT_ref, km_ref):
    pu, pv, qt, kt, vbt = projs
    nchunk = SUB1 // CHUNK
    rows = slice(t * SUB1, (t + 1) * SUB1)

    u = jax.nn.gelu(pu)
    v = jax.nn.gelu(pv)
    ya = []
    per_head = nchunk * HEAD_DIM
    for h0 in range(0, N_HEADS, MIX_HEADS):
        heads = range(h0, h0 + MIX_HEADS)
        lhs = []
        for hd in heads:
            sl = slice(hd * HEAD_DIM, (hd + 1) * HEAD_DIM)
            vh = v[sl, :]
            mu = jnp.mean(vh, axis=0, keepdims=True)
            vc = vh - mu
            var = jnp.mean(vc * vc, axis=0, keepdims=True)
            vn = (vc * lax.rsqrt(var + EPS) * cols_ref[sl, COL_LN_G:COL_LN_G + 1]
                  + cols_ref[sl, COL_LN_B:COL_LN_B + 1])
            lhs += [vn[:, c * CHUNK:(c + 1) * CHUNK] for c in range(nchunk)]
        lhs = jnp.concatenate(lhs, axis=0).astype(_BF16)
        mixed_all = lax.dot_general(lhs, w_mix[h0 // MIX_HEADS], _NT, preferred_element_type=_F32)
        for i, hd in enumerate(heads):
            sl = slice(hd * HEAD_DIM, (hd + 1) * HEAD_DIM)
            mixed = mixed_all[i * per_head:(i + 1) * per_head, i * CHUNK:(i + 1) * CHUNK]
            mixed = mixed + sb_ref[hd:hd + 1, :]
            mixed = jnp.concatenate(
                [mixed[c * HEAD_DIM:(c + 1) * HEAD_DIM, :] for c in range(nchunk)], axis=1)
            ya.append(_head_rms(u[sl, :] * mixed, cols_ref[sl, COL_A_G:COL_A_G + 1]))
    ya_ref[0, rows, :] = jnp.concatenate(ya, axis=0).T.astype(_BF16)

    qn = jnp.concatenate(
        [_head_rms(qt[hd * HEAD_DIM:(hd + 1) * HEAD_DIM, :], cols_ref[0:HEAD_DIM, COL_Q_G:COL_Q_G + 1])
         for hd in range(N_HEADS)], axis=0) * Q_SCALE
    kn = jnp.concatenate(
        [_head_rms(kt[hd * HEAD_DIM:(hd + 1) * HEAD_DIM, :], cols_ref[0:HEAD_DIM, COL_K_G:COL_K_G + 1])
         for hd in range(N_HEADS)], axis=0)
    k_tok = kn.T
    qT_ref[0, :, rows] = qn.astype(_BF16)
    vT_ref[0, :, rows] = vbt.astype(_BF16)
    k_ref[0, rows, :] = k_tok.astype(_BF16)
    for b in range(SUB1 // BLK):
        kb = k_tok[b * BLK:(b + 1) * BLK, :]
        km_ref[0, pl.ds((blk0 + t) * (SUB1 // BLK) + b, 1), :] = jnp.mean(kb, axis=0, keepdims=True)


def _attn_jobs(qT_ref, k_ref, vT_ref, km_ref, cols_ref, kx_scr, store_tile):
    nb = k_ref.shape[1] // BLK
    frow = lax.broadcasted_iota(jnp.int32, (PAIR, BLK), 0)
    key_pos = lax.broadcasted_iota(jnp.int32, (BLK, BLK), 0)
    qry_pos = lax.broadcasted_iota(jnp.int32, (BLK, BLK), 1)
    causal_bias = jnp.where(key_pos <= qry_pos, 0.0, -jnp.inf)

    km = km_ref[0]
    km_hi = km.astype(_BF16)
    km_lo = (km - km_hi.astype(_F32)).astype(_BF16)
    kx_scr[0:2 * nb, :] = jnp.concatenate([km_hi, km_lo], axis=0)
    kx_scr[2 * nb:, :] = k_ref[0]

    def scores(jj, hh):
        nkeys = (jj + 1) * BLK
        qT = qT_ref[0, :, jj * BLK:(jj + 1) * BLK]
        in_head = (frow >= hh * HEAD_DIM) & (frow < (hh + 1) * HEAD_DIM)
        qm = jnp.where(in_head, qT, jnp.zeros_like(qT))
        bias = None
        if jj > TOPK:
            sx = jnp.dot(kx_scr[0:2 * nb + nkeys, :], qm, preferred_element_type=_F32)
            gate = sx[0:jj, :] + sx[nb:nb + jj, :]
            s = sx[2 * nb:, :]
            blk_id = lax.broadcasted_iota(jnp.int32, (jj, BLK), 0)
            rank = jnp.zeros((jj, BLK), _F32)
            for m in range(jj):
                gm = gate[m:m + 1, :]
                ahead = (gm > gate) | ((gm == gate) & (blk_id > m))
                rank = rank + jnp.where(ahead, 1.0, 0.0)
            bias = jnp.where(rank < TOPK, 0.0, -jnp.inf)
        else:
            s = jnp.dot(kx_scr[2 * nb:2 * nb + nkeys, :], qm, preferred_element_type=_F32)
        blocks, m = [], None
        for n in range(jj + 1):
            sn = s[n * BLK:(n + 1) * BLK, :]
            if n == jj:
                sn = sn + causal_bias
            elif bias is not None:
                sn = sn + bias[n:n + 1, :]
            blocks.append(sn)
            mn = jnp.max(sn, axis=0, keepdims=True)
            m = mn if m is None else jnp.maximum(m, mn)
        return blocks, m

    tiles = {}

    def softmax(jj, hh, blocks, m):
        probs, l = [], None
        for sn in blocks:
            p = jnp.exp2(sn - m)
            ln = jnp.sum(p, axis=0, keepdims=True)
            l = ln if l is None else l + ln
            probs.append(p.astype(_BF16))
        p = jnp.concatenate(probs, axis=0) if jj > 0 else probs[0]
        return jj, hh, p, l

    def pv(jj, hh, p, l):
        vrows = slice(hh * HEAD_DIM, (hh + 1) * HEAD_DIM)
        o = jnp.dot(vT_ref[0, vrows, 0:(jj + 1) * BLK], p, preferred_element_type=_F32)
        tiles.setdefault(jj, []).append(
            _head_rms(o / l, cols_ref[vrows, COL_B_G:COL_B_G + 1]))
        if len(tiles[jj]) == HEADS_PER_STEP:
            y = jnp.concatenate(tiles.pop(jj), axis=0)
            store_tile(jj, y.T.astype(_BF16))

    jobs = [(jj, hh) for jj in range(nb) for hh in range(HEADS_PER_STEP)]
    return jobs, scores, softmax, pv


def _attn_kernel(qT_ref, k_ref, vT_ref, km_ref, cols_ref, yb_ref, kx_scr):
    def store_tile(jj, tile):
        yb_ref[0, jj * BLK:(jj + 1) * BLK, :] = tile

    jobs, scores, softmax, pv = _attn_jobs(qT_ref, k_ref, vT_ref, km_ref, cols_ref, kx_scr, store_tile)
    pending = []
    for job in jobs:
        pending.append((*job, *scores(*job)))
        if len(pending) > SCORES_AHEAD:
            pv(*softmax(*pending.pop(0)))
    for job in pending:
        pv(*softmax(*job))


def _ffn_stages(load_x, load_y, wo_ref, g2_ref, wg_ref, wup_ref, wd_ref, store):
    st = {}

    def outproj():
        st["x1"] = load_x() + jnp.dot(load_y(), wo_ref[...], preferred_element_type=_F32)

    def norm():
        x1 = st["x1"]
        ms = jnp.mean(x1 * x1, axis=-1, keepdims=True)
        st["h"] = (x1 * lax.rsqrt(ms + EPS) * g2_ref[...]).astype(_BF16)

    def gate():
        st["g"] = jnp.dot(st["h"], wg_ref[...], preferred_element_type=_F32)

    def up():
        st["u"] = jnp.dot(st.pop("h"), wup_ref[...], preferred_element_type=_F32)

    def act():
        st["a"] = (jax.nn.silu(st.pop("g")) * st.pop("u")).astype(_BF16)

    def down():
        store(st.pop("x1") + jnp.dot(st.pop("a"), wd_ref[...], preferred_element_type=_F32))

    return outproj, norm, gate, up, act, down


def _ffn_kernel(x_ref, ya_ref, yb_ref, wo_ref, g2_ref, wg_ref, wup_ref, wd_ref, _aliased_out, o_ref):
    def store(val):
        o_ref[0] = val

    for stage in _ffn_stages(lambda: x_ref[0],
                             lambda: jnp.concatenate([ya_ref[0], yb_ref[0]], axis=1),
                             wo_ref, g2_ref, wg_ref, wup_ref, wd_ref, store):
        stage()


def _attn_ffn_kernel(qT_ref, k_ref, vT_ref, km_ref, cols_ref,
                     x_ref, ya_ref, yb0_ref, wo_ref, g2_ref, wg_ref, wup_ref, wd_ref,
                     yb_ref, o_ref, kx_scr, yb_scr):
    r, p = pl.program_id(0), pl.program_id(1)
    n_pair = yb_scr.shape[1]

    @pl.when((r == 0) & (p == 0))
    def _():
        for q in range(n_pair):
            yb_scr[0, q] = yb0_ref[0, :, q * PAIR:(q + 1) * PAIR]

    rd = r % 2
    wr = 1 - rd

    def store_tile(jj, tile):
        yb_ref[0, jj * BLK:(jj + 1) * BLK, :] = tile
        yb_scr[wr, p, jj * BLK:(jj + 1) * BLK, :] = tile

    jobs, scores, softmax, pv = _attn_jobs(qT_ref, k_ref, vT_ref, km_ref, cols_ref, kx_scr, store_tile)

    subs = []
    for t in range(TM3 // SUB3):
        rows = slice(t * SUB3, (t + 1) * SUB3)

        def load_y(t=t, rows=rows):
            tok = pl.ds(pl.multiple_of(p * TM3 + t * SUB3, SUB3), SUB3)
            return jnp.concatenate(
                [ya_ref[0, rows, :]] + [yb_scr[rd, q, tok, :] for q in range(n_pair)], axis=1)

        def store(val, rows=rows):
            o_ref[0, rows, :] = val

        subs.append(_ffn_stages(lambda rows=rows: x_ref[0, rows, :], load_y,
                                wo_ref, g2_ref, wg_ref, wup_ref, wd_ref, store))
    (a_out, a_norm, a_gate, a_up, a_act, a_down), (b_out, b_norm, b_gate, b_up, b_act, b_down) = subs
    slots = [(a_out,), (b_out, a_norm), (a_gate, b_norm), (a_up, a_act),
             (b_gate,), (b_up, b_act), (a_down,), (b_down,)]
    per_slot = len(jobs) // len(slots)
    assert per_slot * len(slots) == len(jobs)
    for i, slot in enumerate(slots):
        mine = jobs[i * per_slot:(i + 1) * per_slot]
        started = [(*job, *scores(*job)) for job in mine]
        weighted = [softmax(*job) for job in started]
        for stage in slot:
            stage()
        for job in weighted:
            pv(*job)


def _full(shape):
    return pl.BlockSpec(shape, lambda *_: (0,) * len(shape), pipeline_mode=pl.Buffered(1))


def _layer(x, norm1_g, w_in, sgu_ln_g, sgu_ln_b, sgu_w, sgu_b, q_norm_g, k_norm_g,
           out_norm_a_g, out_norm_b_g, w_out, norm2_g, w_gate, w_up, w_down):
    B, S, D = x.shape
    nb = S // BLK
    assert D == D_MODEL and S % TM1 == 0

    w_in_t = w_in.astype(_BF16).T
    cols = [None] * N_COLS
    cols[COL_LN_G], cols[COL_LN_B] = sgu_ln_g, sgu_ln_b
    cols[COL_A_G], cols[COL_B_G] = out_norm_a_g, out_norm_b_g
    cols[COL_Q_G], cols[COL_K_G] = jnp.tile(q_norm_g, N_HEADS), jnp.tile(k_norm_g, N_HEADS)
    cols = jnp.stack(cols, axis=1).astype(_F32)

    steps1 = B * (S // TM1)
    ffn_w = (w_out, w_gate, w_up, w_down)
    assert all(w.shape[0] % (16 * steps1) == 0 for w in ffn_w)
    slab_specs = [pl.BlockSpec((w.shape[0] // steps1, w.shape[1]), lambda b, s: (b * (S // TM1) + s, 0))
                  for w in ffn_w]

    ya, qT, k, vT, km, wo_b, wg_b, wup_b, wd_b = pl.pallas_call(
        _inproj_kernel,
        grid=(B, S // TM1),
        in_specs=[
            pl.BlockSpec((1, TM1, D), lambda b, s: (b, s, 0)),
            _full((1, D)),
            _full(w_in_t.shape),
            _full((WIDTH, N_COLS)),
            _full((N_HEADS, CHUNK, CHUNK)), _full((N_HEADS, CHUNK)),
            *slab_specs,
        ],
        out_specs=[
            pl.BlockSpec((1, TM1, WIDTH), lambda b, s: (b, s, 0)),
            pl.BlockSpec((1, WIDTH, TM1), lambda b, s: (b, 0, s)),
            pl.BlockSpec((1, TM1, WIDTH), lambda b, s: (b, s, 0)),
            pl.BlockSpec((1, WIDTH, TM1), lambda b, s: (b, 0, s)),
            pl.BlockSpec((1, nb, WIDTH), lambda b, s: (b, 0, 0)),
            *slab_specs,
        ],
        out_shape=[
            jax.ShapeDtypeStruct((B, S, WIDTH), _BF16),
            jax.ShapeDtypeStruct((B, WIDTH, S), _BF16),
            jax.ShapeDtypeStruct((B, S, WIDTH), _BF16),
            jax.ShapeDtypeStruct((B, WIDTH, S), _BF16),
            jax.ShapeDtypeStruct((B, nb, WIDTH), _F32),
            *[jax.ShapeDtypeStruct(w.shape, _BF16) for w in ffn_w],
        ],
        compiler_params=pltpu.CompilerParams(
            dimension_semantics=("parallel", "arbitrary"),
            vmem_limit_bytes=VMEM_LIMIT_BYTES),
        name="inproj_sgu",
    )(x, norm1_g.reshape(1, D), w_in_t, cols, sgu_w, sgu_b, *ffn_w)

    n_pair = WIDTH // PAIR
    assert S == n_pair * TM3 and B >= 2

    def attn_specs(row):
        return [
            pl.BlockSpec((1, PAIR, S), lambda r, p: (row(r), p, 0)),
            pl.BlockSpec((1, S, PAIR), lambda r, p: (row(r), 0, p)),
            pl.BlockSpec((1, PAIR, S), lambda r, p: (row(r), p, 0)),
            pl.BlockSpec((1, nb, PAIR), lambda r, p: (row(r), 0, p)),
            pl.BlockSpec((PAIR, N_COLS), lambda r, p: (p, 0)),
        ]

    kx_scratch = pltpu.VMEM((2 * nb + S, PAIR), _BF16)
    ffn_w_specs = [_full((2 * WIDTH, D)), _full((1, D)), _full((D, D_FF)), _full((D, D_FF)), _full((D_FF, D))]
    ffn_w_args = (wo_b, norm2_g.reshape(1, D), wg_b, wup_b, wd_b)

    yb0 = pl.pallas_call(
        _attn_kernel,
        grid=(1, n_pair),
        in_specs=attn_specs(lambda r: r),
        out_specs=pl.BlockSpec((1, S, PAIR), lambda r, p: (r, 0, p)),
        out_shape=jax.ShapeDtypeStruct((1, S, WIDTH), _BF16),
        scratch_shapes=[kx_scratch],
        compiler_params=pltpu.CompilerParams(
            dimension_semantics=("parallel", "parallel"),
            vmem_limit_bytes=VMEM_LIMIT_BYTES),
        name="moba_attn",
    )(qT, k, vT, km, cols)

    yb, out = pl.pallas_call(
        _attn_ffn_kernel,
        grid=(B - 1, n_pair),
        in_specs=[
            *attn_specs(lambda r: r + 1),
            pl.BlockSpec((1, TM3, D), lambda r, p: (r, p, 0)),
            pl.BlockSpec((1, TM3, WIDTH), lambda r, p: (r, p, 0)),
            _full((1, S, WIDTH)),
            *ffn_w_specs,
        ],
        out_specs=[
            pl.BlockSpec((1, S, PAIR), lambda r, p: (0, 0, p)),
            pl.BlockSpec((1, TM3, D), lambda r, p: (r, p, 0)),
        ],
        out_shape=[
            jax.ShapeDtypeStruct((1, S, WIDTH), _BF16),
            jax.ShapeDtypeStruct((B, S, D), _F32),
        ],
        scratch_shapes=[kx_scratch, pltpu.VMEM((2, n_pair, S, PAIR), _BF16)],
        compiler_params=pltpu.CompilerParams(
            dimension_semantics=("arbitrary", "arbitrary"),
            vmem_limit_bytes=VMEM_LIMIT_BYTES),
        name="attn_ffn",
    )(qT, k, vT, km, cols, x, ya, yb0, *ffn_w_args)

    last = B - 1
    out = pl.pallas_call(
        _ffn_kernel,
        grid=(n_pair,),
        in_specs=[
            pl.BlockSpec((1, TM3, D), lambda i: (last, i, 0)),
            pl.BlockSpec((1, TM3, WIDTH), lambda i: (last, i, 0)),
            pl.BlockSpec((1, TM3, WIDTH), lambda i: (0, i, 0)),
            *ffn_w_specs,
            pl.BlockSpec(memory_space=pl.ANY),
        ],
        out_specs=pl.BlockSpec((1, TM3, D), lambda i: (last, i, 0)),
        out_shape=jax.ShapeDtypeStruct((B, S, D), _F32),
        input_output_aliases={8: 0},
        compiler_params=pltpu.CompilerParams(
            dimension_semantics=("parallel",),
            vmem_limit_bytes=VMEM_LIMIT_BYTES),
        name="outproj_ffn",
    )(x, ya, yb, *ffn_w_args, out)
    return out


def kernel(x, norm1_g, w_in, sgu_ln_g, sgu_ln_b, sgu_w, sgu_b, q_norm_g, k_norm_g,
           out_norm_a_g, out_norm_b_g, w_out, norm2_g, w_gate, w_up, w_down):
    depth = norm1_g.shape[0]
    for l in range(depth):
        x = _layer(x, norm1_g[l], w_in[l], sgu_ln_g[l], sgu_ln_b[l], sgu_w[l], sgu_b[l],
                   q_norm_g[l], k_norm_g[l], out_norm_a_g[l], out_norm_b_g[l], w_out[l],
                   norm2_g[l], w_gate[l], w_up[l], w_down[l])
    return x
```

```python
import jax
import jax.numpy as jnp
from jax import lax
from jax.experimental import pallas as pl
from jax.experimental.pallas import tpu as pltpu

D_MODEL = 1024
HEAD_DIM = 64
N_HEADS = 8
WIDTH = N_HEADS * HEAD_DIM
CHUNK = 128
BLK = 256
TOPK = 3
D_FF = 2816
EPS = 1e-6
Q_SCALE = HEAD_DIM ** -0.5 * 1.4426950408889634

TM1 = 1024
SUB1 = 256
MIX_HEADS = 2
TM3 = 512
SUB3 = 256
HEADS_PER_STEP = 2
PAIR = HEADS_PER_STEP * HEAD_DIM
SCORES_AHEAD = 2

COL_LN_G, COL_LN_B, COL_A_G, COL_B_G, COL_Q_G, COL_K_G = range(6)
N_COLS = 6

VMEM_LIMIT_BYTES = 56 * 1024 * 1024
_NT = (((1,), (1,)), ((), ()))
_F32 = jnp.float32
_BF16 = jnp.bfloat16


def _head_rms(t, gain):
    ms = jnp.mean(t * t, axis=0, keepdims=True)
    return t * lax.rsqrt(ms + EPS) * gain


def _inproj_kernel(x_ref, g1_ref, win_ref, cols_ref, sw_ref, sb_ref,
                   wo32_ref, wg32_ref, wup32_ref, wd32_ref,
                   ya_ref, qT_ref, k_ref, vT_ref, km_ref, wo_ref, wg_ref, wup_ref, wd_ref):
    for src, dst in ((wo32_ref, wo_ref), (wg32_ref, wg_ref), (wup32_ref, wup_ref), (wd32_ref, wd_ref)):
        dst[...] = src[...].astype(_BF16)

    si = pl.program_id(1)
    nsub = TM1 // SUB1
    row = lax.broadcasted_iota(jnp.int32, (CHUNK, CHUNK), 0)
    col = lax.broadcasted_iota(jnp.int32, (CHUNK, CHUNK), 1)
    causal = col <= row
    w_mix = [jnp.where(causal, sw_ref[hd], 0.0).astype(_BF16) for hd in range(N_HEADS)]
    w_mix = [jnp.concatenate(w_mix[h0:h0 + MIX_HEADS], axis=0)
             for h0 in range(0, N_HEADS, MIX_HEADS)]

    def project(t):
        x = x_ref[0, t * SUB1:(t + 1) * SUB1, :]
        ms = jnp.mean(x * x, axis=-1, keepdims=True)
        h = (x * lax.rsqrt(ms + EPS) * g1_ref[...]).astype(_BF16)
        return [lax.dot_general(win_ref[i * WIDTH:(i + 1) * WIDTH, :], h, _NT, preferred_element_type=_F32)
                for i in range(5)]

    pending = []
    for t in range(nsub):
        pending.append((t, project(t)))
        if len(pending) > 1:
            _inproj_finish(*pending.pop(0), si * nsub, w_mix, cols_ref, sb_ref,
                           ya_ref, qT_ref, k_ref, vT_ref, km_ref)
    _inproj_finish(*pending.pop(0), si * nsub, w_mix, cols_ref, sb_ref,
                   ya_ref, qT_ref, k_ref, vT_ref, km_ref)


def _inproj_finish(t, projs, blk0, w_mix, cols_ref, sb_ref, ya_ref, qT_ref, k_ref, vT_ref, km_ref):
    pu, pv, qt, kt, vbt = projs
    nchunk = SUB1 // CHUNK
    rows = slice(t * SUB1, (t + 1) * SUB1)

    u = jax.nn.gelu(pu)
    v = jax.nn.gelu(pv)
    ya = []
    per_head = nchunk * HEAD_DIM
    for h0 in range(0, N_HEADS, MIX_HEADS):
        heads = range(h0, h0 + MIX_HEADS)
        lhs = []
        for hd in heads:
            sl = slice(hd * HEAD_DIM, (hd + 1) * HEAD_DIM)
            vh = v[sl, :]
            mu = jnp.mean(vh, axis=0, keepdims=True)
            vc = vh - mu
            var = jnp.mean(vc * vc, axis=0, keepdims=True)
            vn = (vc * lax.rsqrt(var + EPS) * cols_ref[sl, COL_LN_G:COL_LN_G + 1]
                  + cols_ref[sl, COL_LN_B:COL_LN_B + 1])
            lhs += [vn[:, c * CHUNK:(c + 1) * CHUNK] for c in range(nchunk)]
        lhs = jnp.concatenate(lhs, axis=0).astype(_BF16)
        mixed_all = lax.dot_general(lhs, w_mix[h0 // MIX_HEADS], _NT, preferred_element_type=_F32)
        for i, hd in enumerate(heads):
            sl = slice(hd * HEAD_DIM, (hd + 1) * HEAD_DIM)
            mixed = mixed_all[i * per_head:(i + 1) * per_head, i * CHUNK:(i + 1) * CHUNK]
            mixed = mixed + sb_ref[hd:hd + 1, :]
            mixed = jnp.concatenate(
                [mixed[c * HEAD_DIM:(c + 1) * HEAD_DIM, :] for c in range(nchunk)], axis=1)
            ya.append(_head_rms(u[sl, :] * mixed, cols_ref[sl, COL_A_G:COL_A_G + 1]))
    ya_ref[0, rows, :] = jnp.concatenate(ya, axis=0).T.astype(_BF16)

    qn = jnp.concatenate(
        [_head_rms(qt[hd * HEAD_DIM:(hd + 1) * HEAD_DIM, :], cols_ref[0:HEAD_DIM, COL_Q_G:COL_Q_G + 1])
         for hd in range(N_HEADS)], axis=0) * Q_SCALE
    kn = jnp.concatenate(
        [_head_rms(kt[hd * HEAD_DIM:(hd + 1) * HEAD_DIM, :], cols_ref[0:HEAD_DIM, COL_K_G:COL_K_G + 1])
         for hd in range(N_HEADS)], axis=0)
    k_tok = kn.T
    qT_ref[0, :, rows] = qn.astype(_BF16)
    vT_ref[0, :, rows] = vbt.astype(_BF16)
    k_ref[0, rows, :] = k_tok.astype(_BF16)
    for b in range(SUB1 // BLK):
        kb = k_tok[b * BLK:(b + 1) * BLK, :]
        km_ref[0, pl.ds((blk0 + t) * (SUB1 // BLK) + b, 1), :] = jnp.mean(kb, axis=0, keepdims=True)


def _attn_jobs(qT_ref, k_ref, vT_ref, km_ref, cols_ref, kx_scr, store_tile):
    nb = k_ref.shape[1] // BLK
    frow = lax.broadcasted_iota(jnp.int32, (PAIR, BLK), 0)
    key_pos = lax.broadcasted_iota(jnp.int32, (BLK, BLK), 0)
    qry_pos = lax.broadcasted_iota(jnp.int32, (BLK, BLK), 1)
    causal_bias = jnp.where(key_pos <= qry_pos, 0.0, -jnp.inf)

    km = km_ref[0]
    km_hi = km.astype(_BF16)
    km_lo = (km - km_hi.astype(_F32)).astype(_BF16)
    kx_scr[0:2 * nb, :] = jnp.concatenate([km_hi, km_lo], axis=0)
    kx_scr[2 * nb:, :] = k_ref[0]

    def scores(jj, hh):
        nkeys = (jj + 1) * BLK
        qT = qT_ref[0, :, jj * BLK:(jj + 1) * BLK]
        in_head = (frow >= hh * HEAD_DIM) & (frow < (hh + 1) * HEAD_DIM)
        qm = jnp.where(in_head, qT, jnp.zeros_like(qT))
        bias = None
        if jj > TOPK:
            sx = jnp.dot(kx_scr[0:2 * nb + nkeys, :], qm, preferred_element_type=_F32)
            gate = sx[0:jj, :] + sx[nb:nb + jj, :]
            s = sx[2 * nb:, :]
            blk_id = lax.broadcasted_iota(jnp.int32, (jj, BLK), 0)
            rank = jnp.zeros((jj, BLK), _F32)
            for m in range(jj):
                gm = gate[m:m + 1, :]
                ahead = (gm > gate) | ((gm == gate) & (blk_id > m))
                rank = rank + jnp.where(ahead, 1.0, 0.0)
            bias = jnp.where(rank < TOPK, 0.0, -jnp.inf)
        else:
            s = jnp.dot(k_ref[0, 0:nkeys, :], qm, preferred_element_type=_F32)
        blocks, m = [], None
        for n in range(jj + 1):
            sn = s[n * BLK:(n + 1) * BLK, :]
            if n == jj:
                sn = sn + causal_bias
            elif bias is not None:
                sn = sn + bias[n:n + 1, :]
            blocks.append(sn)
            mn = jnp.max(sn, axis=0, keepdims=True)
            m = mn if m is None else jnp.maximum(m, mn)
        return blocks, m

    tiles = {}

    def softmax(jj, hh, blocks, m):
        probs, l = [], None
        for sn in blocks:
            p = jnp.exp2(sn - m)
            ln = jnp.sum(p, axis=0, keepdims=True)
            l = ln if l is None else l + ln
            probs.append(p.astype(_BF16))
        p = jnp.concatenate(probs, axis=0) if jj > 0 else probs[0]
        return jj, hh, p, l

    def pv(jj, hh, p, l):
        vrows = slice(hh * HEAD_DIM, (hh + 1) * HEAD_DIM)
        o = jnp.dot(vT_ref[0, vrows, 0:(jj + 1) * BLK], p, preferred_element_type=_F32)
        tiles.setdefault(jj, []).append(
            _head_rms(o / l, cols_ref[vrows, COL_B_G:COL_B_G + 1]))
        if len(tiles[jj]) == HEADS_PER_STEP:
            y = jnp.concatenate(tiles.pop(jj), axis=0)
            store_tile(jj, y.T.astype(_BF16))

    jobs = [(jj, hh) for jj in range(nb) for hh in range(HEADS_PER_STEP)]
    return jobs, scores, softmax, pv


def _attn_kernel(qT_ref, k_ref, vT_ref, km_ref, cols_ref, yb_ref, kx_scr):
    def store_tile(jj, tile):
        yb_ref[0, jj * BLK:(jj + 1) * BLK, :] = tile

    jobs, scores, softmax, pv = _attn_jobs(qT_ref, k_ref, vT_ref, km_ref, cols_ref, kx_scr, store_tile)
    pending = []
    for job in jobs:
        pending.append((*job, *scores(*job)))
        if len(pending) > SCORES_AHEAD:
            pv(*softmax(*pending.pop(0)))
    for job in pending:
        pv(*softmax(*job))


def _ffn_stages(load_x, load_y, wo_ref, g2_ref, wg_ref, wup_ref, wd_ref, store):
    st = {}

    def outproj():
        st["x1"] = load_x() + jnp.dot(load_y(), wo_ref[...], preferred_element_type=_F32)

    def norm():
        x1 = st["x1"]
        ms = jnp.mean(x1 * x1, axis=-1, keepdims=True)
        st["h"] = (x1 * lax.rsqrt(ms + EPS) * g2_ref[...]).astype(_BF16)

    def gate():
        st["g"] = jnp.dot(st["h"], wg_ref[...], preferred_element_type=_F32)

    def up():
        st["u"] = jnp.dot(st.pop("h"), wup_ref[...], preferred_element_type=_F32)

    def act():
        st["a"] = (jax.nn.silu(st.pop("g")) * st.pop("u")).astype(_BF16)

    def down():
        store(st.pop("x1") + jnp.dot(st.pop("a"), wd_ref[...], preferred_element_type=_F32))

    return outproj, norm, gate, up, act, down


def _ffn_kernel(x_ref, ya_ref, yb_ref, wo_ref, g2_ref, wg_ref, wup_ref, wd_ref, _aliased_out, o_ref):
    def store(val):
        o_ref[0] = val

    for stage in _ffn_stages(lambda: x_ref[0],
                             lambda: jnp.concatenate([ya_ref[0], yb_ref[0]], axis=1),
                             wo_ref, g2_ref, wg_ref, wup_ref, wd_ref, store):
        stage()


def _attn_ffn_kernel(qT_ref, k_ref, vT_ref, km_ref, cols_ref,
                     x_ref, ya_ref, yb0_ref, wo_ref, g2_ref, wg_ref, wup_ref, wd_ref,
                     yb_ref, o_ref, kx_scr, yb_scr):
    r, p = pl.program_id(0), pl.program_id(1)
    n_pair = yb_scr.shape[1]

    @pl.when((r == 0) & (p == 0))
    def _():
        for q in range(n_pair):
            yb_scr[0, q] = yb0_ref[0, :, q * PAIR:(q + 1) * PAIR]

    rd = r % 2
    wr = 1 - rd

    def store_tile(jj, tile):
        yb_ref[0, jj * BLK:(jj + 1) * BLK, :] = tile
        yb_scr[wr, p, jj * BLK:(jj + 1) * BLK, :] = tile

    jobs, scores, softmax, pv = _attn_jobs(qT_ref, k_ref, vT_ref, km_ref, cols_ref, kx_scr, store_tile)

    subs = []
    for t in range(TM3 // SUB3):
        rows = slice(t * SUB3, (t + 1) * SUB3)

        def load_y(t=t, rows=rows):
            tok = pl.ds(pl.multiple_of(p * TM3 + t * SUB3, SUB3), SUB3)
            return jnp.concatenate(
                [ya_ref[0, rows, :]] + [yb_scr[rd, q, tok, :] for q in range(n_pair)], axis=1)

        def store(val, rows=rows):
            o_ref[0, rows, :] = val

        subs.append(_ffn_stages(lambda rows=rows: x_ref[0, rows, :], load_y,
                                wo_ref, g2_ref, wg_ref, wup_ref, wd_ref, store))
    (a_out, a_norm, a_gate, a_up, a_act, a_down), (b_out, b_norm, b_gate, b_up, b_act, b_down) = subs
    slots = [(a_out,), (b_out, a_norm), (a_gate, b_norm), (a_up, a_act),
             (b_gate,), (b_up, b_act), (a_down,), (b_down,)]
    per_slot = len(jobs) // len(slots)
    assert per_slot * len(slots) == len(jobs)
    jobs = jobs[per_slot:] + jobs[:per_slot]
    for i, slot in enumerate(slots):
        mine = jobs[i * per_slot:(i + 1) * per_slot]
        started = [(*job, *scores(*job)) for job in mine]
        weighted = [softmax(*job) for job in started]
        for stage in slot:
            stage()
        for job in weighted:
            pv(*job)


def _full(shape):
    return pl.BlockSpec(shape, lambda *_: (0,) * len(shape), pipeline_mode=pl.Buffered(1))


def _layer(x, norm1_g, w_in, sgu_ln_g, sgu_ln_b, sgu_w, sgu_b, q_norm_g, k_norm_g,
           out_norm_a_g, out_norm_b_g, w_out, norm2_g, w_gate, w_up, w_down):
    B, S, D = x.shape
    nb = S // BLK
    assert D == D_MODEL and S % TM1 == 0

    w_in_t = w_in.astype(_BF16).T
    cols = [None] * N_COLS
    cols[COL_LN_G], cols[COL_LN_B] = sgu_ln_g, sgu_ln_b
    cols[COL_A_G], cols[COL_B_G] = out_norm_a_g, out_norm_b_g
    cols[COL_Q_G], cols[COL_K_G] = jnp.tile(q_norm_g, N_HEADS), jnp.tile(k_norm_g, N_HEADS)
    cols = jnp.stack(cols, axis=1).astype(_F32)

    steps1 = B * (S // TM1)
    ffn_w = (w_out, w_gate, w_up, w_down)
    assert all(w.shape[0] % (16 * steps1) == 0 for w in ffn_w)
    slab_specs = [pl.BlockSpec((w.shape[0] // steps1, w.shape[1]), lambda b, s: (b * (S // TM1) + s, 0))
                  for w in ffn_w]

    ya, qT, k, vT, km, wo_b, wg_b, wup_b, wd_b = pl.pallas_call(
        _inproj_kernel,
        grid=(B, S // TM1),
        in_specs=[
            pl.BlockSpec((1, TM1, D), lambda b, s: (b, s, 0)),
            _full((1, D)),
            _full(w_in_t.shape),
            _full((WIDTH, N_COLS)),
            _full((N_HEADS, CHUNK, CHUNK)), _full((N_HEADS, CHUNK)),
            *slab_specs,
        ],
        out_specs=[
            pl.BlockSpec((1, TM1, WIDTH), lambda b, s: (b, s, 0)),
            pl.BlockSpec((1, WIDTH, TM1), lambda b, s: (b, 0, s)),
            pl.BlockSpec((1, TM1, WIDTH), lambda b, s: (b, s, 0)),
            pl.BlockSpec((1, WIDTH, TM1), lambda b, s: (b, 0, s)),
            pl.BlockSpec((1, nb, WIDTH), lambda b, s: (b, 0, 0)),
            *slab_specs,
        ],
        out_shape=[
            jax.ShapeDtypeStruct((B, S, WIDTH), _BF16),
            jax.ShapeDtypeStruct((B, WIDTH, S), _BF16),
            jax.ShapeDtypeStruct((B, S, WIDTH), _BF16),
            jax.ShapeDtypeStruct((B, WIDTH, S), _BF16),
            jax.ShapeDtypeStruct((B, nb, WIDTH), _F32),
            *[jax.ShapeDtypeStruct(w.shape, _BF16) for w in ffn_w],
        ],
        compiler_params=pltpu.CompilerParams(
            dimension_semantics=("parallel", "arbitrary"),
            vmem_limit_bytes=VMEM_LIMIT_BYTES),
        name="inproj_sgu",
    )(x, norm1_g.reshape(1, D), w_in_t, cols, sgu_w, sgu_b, *ffn_w)

    n_pair = WIDTH // PAIR
    assert S == n_pair * TM3 and B >= 2

    def attn_specs(row):
        return [
            pl.BlockSpec((1, PAIR, S), lambda r, p: (row(r), p, 0)),
            pl.BlockSpec((1, S, PAIR), lambda r, p: (row(r), 0, p)),
            pl.BlockSpec((1, PAIR, S), lambda r, p: (row(r), p, 0)),
            pl.BlockSpec((1, nb, PAIR), lambda r, p: (row(r), 0, p)),
            pl.BlockSpec((PAIR, N_COLS), lambda r, p: (p, 0)),
        ]

    kx_scratch = pltpu.VMEM((2 * nb + S, PAIR), _BF16)
    ffn_w_specs = [_full((2 * WIDTH, D)), _full((1, D)), _full((D, D_FF)), _full((D, D_FF)), _full((D_FF, D))]
    ffn_w_args = (wo_b, norm2_g.reshape(1, D), wg_b, wup_b, wd_b)

    yb0 = pl.pallas_call(
        _attn_kernel,
        grid=(1, n_pair),
        in_specs=attn_specs(lambda r: r),
        out_specs=pl.BlockSpec((1, S, PAIR), lambda r, p: (r, 0, p)),
        out_shape=jax.ShapeDtypeStruct((1, S, WIDTH), _BF16),
        scratch_shapes=[kx_scratch],
        compiler_params=pltpu.CompilerParams(
            dimension_semantics=("parallel", "parallel"),
            vmem_limit_bytes=VMEM_LIMIT_BYTES),
        name="moba_attn",
    )(qT, k, vT, km, cols)

    yb, out = pl.pallas_call(
        _attn_ffn_kernel,
        grid=(B - 1, n_pair),
        in_specs=[
            *attn_specs(lambda r: r + 1),
            pl.BlockSpec((1, TM3, D), lambda r, p: (r, p, 0)),
            pl.BlockSpec((1, TM3, WIDTH), lambda r, p: (r, p, 0)),
            _full((1, S, WIDTH)),
            *ffn_w_specs,
        ],
        out_specs=[
            pl.BlockSpec((1, S, PAIR), lambda r, p: (r, 0, p)),
            pl.BlockSpec((1, TM3, D), lambda r, p: (r, p, 0)),
        ],
        out_shape=[
            jax.ShapeDtypeStruct((B - 1, S, WIDTH), _BF16),
            jax.ShapeDtypeStruct((B, S, D), _F32),
        ],
        scratch_shapes=[kx_scratch, pltpu.VMEM((2, n_pair, S, PAIR), _BF16)],
        compiler_params=pltpu.CompilerParams(
            dimension_semantics=("arbitrary", "arbitrary"),
            vmem_limit_bytes=VMEM_LIMIT_BYTES),
        name="attn_ffn",
    )(qT, k, vT, km, cols, x, ya, yb0, *ffn_w_args)

    last = B - 1
    out = pl.pallas_call(
        _ffn_kernel,
        grid=(n_pair,),
        in_specs=[
            pl.BlockSpec((1, TM3, D), lambda i: (last, i, 0)),
            pl.BlockSpec((1, TM3, WIDTH), lambda i: (last, i, 0)),
            pl.BlockSpec((1, TM3, WIDTH), lambda i: (last - 1, i, 0)),
            *ffn_w_specs,
            pl.BlockSpec(memory_space=pl.ANY),
        ],
        out_specs=pl.BlockSpec((1, TM3, D), lambda i: (last, i, 0)),
        out_shape=jax.ShapeDtypeStruct((B, S, D), _F32),
        input_output_aliases={8: 0},
        compiler_params=pltpu.CompilerParams(
            dimension_semantics=("parallel",),
            vmem_limit_bytes=VMEM_LIMIT_BYTES),
        name="outproj_ffn",
    )(x, ya, yb, *ffn_w_args, out)
    return out


def kernel(x, norm1_g, w_in, sgu_ln_g, sgu_ln_b, sgu_w, sgu_b, q_norm_g, k_norm_g,
           out_norm_a_g, out_norm_b_g, w_out, norm2_g, w_gate, w_up, w_down):
    depth = norm1_g.shape[0]
    for l in range(depth):
        x = _layer(x, norm1_g[l], w_in[l], sgu_ln_g[l], sgu_ln_b[l], sgu_w[l], sgu_b[l],
                   q_norm_g[l], k_norm_g[l], out_norm_a_g[l], out_norm_b_g[l], w_out[l],
                   norm2_g[l], w_gate[l], w_up[l], w_down[l])
    return x
```

```python
import jax
import jax.numpy as jnp
from jax import lax
from jax.experimental import pallas as pl
from jax.experimental.pallas import tpu as pltpu

D_MODEL = 1024
HEAD_DIM = 64
N_HEADS = 8
WIDTH = N_HEADS * HEAD_DIM
CHUNK = 128
BLK = 256
TOPK = 3
D_FF = 2816
EPS = 1e-6
Q_SCALE = HEAD_DIM ** -0.5 * 1.4426950408889634

TM1 = 1024
SUB1 = 256
MIX_HEADS = 2
TM3 = 512
SUB3 = 512
HEADS_PER_STEP = 2
PAIR = HEADS_PER_STEP * HEAD_DIM
SCORES_AHEAD = 2

COL_LN_G, COL_LN_B, COL_A_G, COL_B_G, COL_Q_G, COL_K_G = range(6)
N_COLS = 6

VMEM_LIMIT_BYTES = 56 * 1024 * 1024
_NT = (((1,), (1,)), ((), ()))
_F32 = jnp.float32
_BF16 = jnp.bfloat16


def _head_rms(t, gain):
    ms = jnp.mean(t * t, axis=0, keepdims=True)
    return t * lax.rsqrt(ms + EPS) * gain


def _inproj_kernel(x_ref, g1_ref, win_ref, cols_ref, sw_ref, sb_ref,
                   wo32_ref, wg32_ref, wup32_ref, wd32_ref,
                   ya_ref, qT_ref, k_ref, vT_ref, km_ref, wo_ref, wg_ref, wup_ref, wd_ref):
    for src, dst in ((wo32_ref, wo_ref), (wg32_ref, wg_ref), (wup32_ref, wup_ref), (wd32_ref, wd_ref)):
        dst[...] = src[...].astype(_BF16)

    si = pl.program_id(1)
    nsub = TM1 // SUB1
    row = lax.broadcasted_iota(jnp.int32, (CHUNK, CHUNK), 0)
    col = lax.broadcasted_iota(jnp.int32, (CHUNK, CHUNK), 1)
    causal = col <= row
    w_mix = [jnp.where(causal, sw_ref[hd], 0.0).astype(_BF16) for hd in range(N_HEADS)]
    w_mix = [jnp.concatenate(w_mix[h0:h0 + MIX_HEADS], axis=0)
             for h0 in range(0, N_HEADS, MIX_HEADS)]

    def project(t):
        x = x_ref[0, t * SUB1:(t + 1) * SUB1, :]
        ms = jnp.mean(x * x, axis=-1, keepdims=True)
        h = (x * lax.rsqrt(ms + EPS) * g1_ref[...]).astype(_BF16)
        return [lax.dot_general(win_ref[i * WIDTH:(i + 1) * WIDTH, :], h, _NT, preferred_element_type=_F32)
                for i in range(5)]

    pending = []
    for t in range(nsub):
        pending.append((t, project(t)))
        if len(pending) > 1:
            _inproj_finish(*pending.pop(0), si * nsub, w_mix, cols_ref, sb_ref,
                           ya_ref, qT_ref, k_ref, vT_ref, km_ref)
    _inproj_finish(*pending.pop(0), si * nsub, w_mix, cols_ref, sb_ref,
                   ya_ref, qT_ref, k_ref, vT_ref, km_ref)


def _inproj_finish(t, projs, blk0, w_mix, cols_ref, sb_ref, ya_ref, qT_ref, k_ref, vT_ref, km_ref):
    pu, pv, qt, kt, vbt = projs
    nchunk = SUB1 // CHUNK
    rows = slice(t * SUB1, (t + 1) * SUB1)

    u = jax.nn.gelu(pu)
    v = jax.nn.gelu(pv)
    ya = []
    per_head = nchunk * HEAD_DIM
    for h0 in range(0, N_HEADS, MIX_HEADS):
        heads = range(h0, h0 + MIX_HEADS)
        lhs = []
        for hd in heads:
            sl = slice(hd * HEAD_DIM, (hd + 1) * HEAD_DIM)
            vh = v[sl, :]
            mu = jnp.mean(vh, axis=0, keepdims=True)
            vc = vh - mu
            var = jnp.mean(vc * vc, axis=0, keepdims=True)
            vn = (vc * lax.rsqrt(var + EPS) * cols_ref[sl, COL_LN_G:COL_LN_G + 1]
                  + cols_ref[sl, COL_LN_B:COL_LN_B + 1])
            lhs += [vn[:, c * CHUNK:(c + 1) * CHUNK] for c in range(nchunk)]
        lhs = jnp.concatenate(lhs, axis=0).astype(_BF16)
        mixed_all = lax.dot_general(lhs, w_mix[h0 // MIX_HEADS], _NT, preferred_element_type=_F32)
        for i, hd in enumerate(heads):
            sl = slice(hd * HEAD_DIM, (hd + 1) * HEAD_DIM)
            mixed = mixed_all[i * per_head:(i + 1) * per_head, i * CHUNK:(i + 1) * CHUNK]
            mixed = mixed + sb_ref[hd:hd + 1, :]
            mixed = jnp.concatenate(
                [mixed[c * HEAD_DIM:(c + 1) * HEAD_DIM, :] for c in range(nchunk)], axis=1)
            ya.append(_head_rms(u[sl, :] * mixed, cols_ref[sl, COL_A_G:COL_A_G + 1]))
    ya_ref[0, rows, :] = jnp.concatenate(ya, axis=0).T.astype(_BF16)

    qn = jnp.concatenate(
        [_head_rms(qt[hd * HEAD_DIM:(hd + 1) * HEAD_DIM, :], cols_ref[0:HEAD_DIM, COL_Q_G:COL_Q_G + 1])
         for hd in range(N_HEADS)], axis=0) * Q_SCALE
    kn = jnp.concatenate(
        [_head_rms(kt[hd * HEAD_DIM:(hd + 1) * HEAD_DIM, :], cols_ref[0:HEAD_DIM, COL_K_G:COL_K_G + 1])
         for hd in range(N_HEADS)], axis=0)
    k_tok = kn.T
    qT_ref[0, :, rows] = qn.astype(_BF16)
    vT_ref[0, :, rows] = vbt.astype(_BF16)
    k_ref[0, rows, :] = k_tok.astype(_BF16)
    for b in range(SUB1 // BLK):
        kb = k_tok[b * BLK:(b + 1) * BLK, :]
        km_ref[0, pl.ds((blk0 + t) * (SUB1 // BLK) + b, 1), :] = jnp.mean(kb, axis=0, keepdims=True)


def _attn_jobs(qT_ref, k_ref, vT_ref, km_ref, cols_ref, kx_scr, store_tile):
    nb = k_ref.shape[1] // BLK
    frow = lax.broadcasted_iota(jnp.int32, (PAIR, BLK), 0)
    key_pos = lax.broadcasted_iota(jnp.int32, (BLK, BLK), 0)
    qry_pos = lax.broadcasted_iota(jnp.int32, (BLK, BLK), 1)
    causal_bias = jnp.where(key_pos <= qry_pos, 0.0, -jnp.inf)

    km = km_ref[0]
    km_hi = km.astype(_BF16)
    km_lo = (km - km_hi.astype(_F32)).astype(_BF16)
    kx_scr[0:2 * nb, :] = jnp.concatenate([km_hi, km_lo], axis=0)
    kx_scr[2 * nb:, :] = k_ref[0]

    def scores(jj, hh):
        nkeys = (jj + 1) * BLK
        qT = qT_ref[0, :, jj * BLK:(jj + 1) * BLK]
        in_head = (frow >= hh * HEAD_DIM) & (frow < (hh + 1) * HEAD_DIM)
        qm = jnp.where(in_head, qT, jnp.zeros_like(qT))
        bias = None
        if jj > TOPK:
            sx = jnp.dot(kx_scr[0:2 * nb + nkeys, :], qm, preferred_element_type=_F32)
            gate = sx[0:jj, :] + sx[nb:nb + jj, :]
            s = sx[2 * nb:, :]
            blk_id = lax.broadcasted_iota(jnp.int32, (jj, BLK), 0)
            rank = jnp.zeros((jj, BLK), _F32)
            for m in range(jj):
                gm = gate[m:m + 1, :]
                ahead = (gm > gate) | ((gm == gate) & (blk_id > m))
                rank = rank + jnp.where(ahead, 1.0, 0.0)
            bias = jnp.where(rank < TOPK, 0.0, -jnp.inf)
        else:
            s = jnp.dot(kx_scr[2 * nb:2 * nb + nkeys, :], qm, preferred_element_type=_F32)
        blocks, m = [], None
        for n in range(jj + 1):
            sn = s[n * BLK:(n + 1) * BLK, :]
            if n == jj:
                sn = sn + causal_bias
            elif bias is not None:
                sn = sn + bias[n:n + 1, :]
            blocks.append(sn)
            mn = jnp.max(sn, axis=0, keepdims=True)
            m = mn if m is None else jnp.maximum(m, mn)
        return blocks, m

    tiles = {}

    def softmax(jj, hh, blocks, m):
        probs, l = [], None
        for sn in blocks:
            p = jnp.exp2(sn - m)
            ln = jnp.sum(p, axis=0, keepdims=True)
            l = ln if l is None else l + ln
            probs.append(p.astype(_BF16))
        p = jnp.concatenate(probs, axis=0) if jj > 0 else probs[0]
        return jj, hh, p, l

    def pv(jj, hh, p, l):
        vrows = slice(hh * HEAD_DIM, (hh + 1) * HEAD_DIM)
        o = jnp.dot(vT_ref[0, vrows, 0:(jj + 1) * BLK], p, preferred_element_type=_F32)
        tiles.setdefault(jj, []).append(
            _head_rms(o / l, cols_ref[vrows, COL_B_G:COL_B_G + 1]))
        if len(tiles[jj]) == HEADS_PER_STEP:
            y = jnp.concatenate(tiles.pop(jj), axis=0)
            store_tile(jj, y.T.astype(_BF16))

    jobs = [(jj, hh) for jj in range(nb) for hh in range(HEADS_PER_STEP)]
    return jobs, scores, softmax, pv


def _attn_kernel(qT_ref, k_ref, vT_ref, km_ref, cols_ref, yb_ref, kx_scr):
    def store_tile(jj, tile):
        yb_ref[0, jj * BLK:(jj + 1) * BLK, :] = tile

    jobs, scores, softmax, pv = _attn_jobs(qT_ref, k_ref, vT_ref, km_ref, cols_ref, kx_scr, store_tile)
    pending = []
    for job in jobs:
        pending.append((*job, *scores(*job)))
        if len(pending) > SCORES_AHEAD:
            pv(*softmax(*pending.pop(0)))
    for job in pending:
        pv(*softmax(*job))


def _ffn_stages(load_x, load_y, wo_ref, g2_ref, wg_ref, wup_ref, wd_ref, store):
    st = {}

    def outproj():
        st["x1"] = load_x() + jnp.dot(load_y(), wo_ref[...], preferred_element_type=_F32)

    def norm():
        x1 = st["x1"]
        ms = jnp.mean(x1 * x1, axis=-1, keepdims=True)
        st["h"] = (x1 * lax.rsqrt(ms + EPS) * g2_ref[...]).astype(_BF16)

    def gate():
        st["g"] = jnp.dot(st["h"], wg_ref[...], preferred_element_type=_F32)

    def up():
        st["u"] = jnp.dot(st.pop("h"), wup_ref[...], preferred_element_type=_F32)

    def act():
        st["a"] = (jax.nn.silu(st.pop("g")) * st.pop("u")).astype(_BF16)

    def down():
        store(st.pop("x1") + jnp.dot(st.pop("a"), wd_ref[...], preferred_element_type=_F32))

    return outproj, norm, gate, up, act, down


def _ffn_kernel(x_ref, ya_ref, yb_ref, wo_ref, g2_ref, wg_ref, wup_ref, wd_ref, _aliased_out, o_ref):
    def store(val):
        o_ref[0] = val

    for stage in _ffn_stages(lambda: x_ref[0],
                             lambda: jnp.concatenate([ya_ref[0], yb_ref[0]], axis=1),
                             wo_ref, g2_ref, wg_ref, wup_ref, wd_ref, store):
        stage()


def _attn_ffn_kernel(qT_ref, k_ref, vT_ref, km_ref, cols_ref,
                     x_ref, ya_ref, yb0_ref, wo_ref, g2_ref, wg_ref, wup_ref, wd_ref,
                     yb_ref, o_ref, kx_scr, yb_scr):
    r, p = pl.program_id(0), pl.program_id(1)
    n_pair = yb_scr.shape[1]

    @pl.when((r == 0) & (p == 0))
    def _():
        for q in range(n_pair):
            yb_scr[0, q] = yb0_ref[0, :, q * PAIR:(q + 1) * PAIR]

    rd = r % 2
    wr = 1 - rd

    def store_tile(jj, tile):
        yb_ref[0, jj * BLK:(jj + 1) * BLK, :] = tile
        yb_scr[wr, p, jj * BLK:(jj + 1) * BLK, :] = tile

    jobs, scores, softmax, pv = _attn_jobs(qT_ref, k_ref, vT_ref, km_ref, cols_ref, kx_scr, store_tile)

    subs = []
    for t in range(TM3 // SUB3):
        rows = slice(t * SUB3, (t + 1) * SUB3)

        def load_y(t=t, rows=rows):
            tok = pl.ds(pl.multiple_of(p * TM3 + t * SUB3, SUB3), SUB3)
            return jnp.concatenate(
                [ya_ref[0, rows, :]] + [yb_scr[rd, q, tok, :] for q in range(n_pair)], axis=1)

        def store(val, rows=rows):
            o_ref[0, rows, :] = val

        subs.append(_ffn_stages(lambda rows=rows: x_ref[0, rows, :], load_y,
                                wo_ref, g2_ref, wg_ref, wup_ref, wd_ref, store))
    if len(subs) == 2:
        (a_out, a_norm, a_gate, a_up, a_act, a_down), (b_out, b_norm, b_gate, b_up, b_act, b_down) = subs
        slots = [(a_out,), (b_out, a_norm), (a_gate, b_norm), (a_up, a_act),
                 (b_gate,), (b_up, b_act), (a_down,), (b_down,)]
    else:
        (a_out, a_norm, a_gate, a_up, a_act, a_down), = subs
        slots = [(a_out,), (a_norm, a_gate), (a_up,), (a_act, a_down)]
    per_slot = len(jobs) // len(slots)
    assert per_slot * len(slots) == len(jobs)
    for i, slot in enumerate(slots):
        mine = jobs[i * per_slot:(i + 1) * per_slot]
        started = [(*job, *scores(*job)) for job in mine]
        weighted = [softmax(*job) for job in started]
        for stage in slot:
            stage()
        for job in weighted:
            pv(*job)


def _full(shape):
    return pl.BlockSpec(shape, lambda *_: (0,) * len(shape), pipeline_mode=pl.Buffered(1))


def _layer(x, norm1_g, w_in, sgu_ln_g, sgu_ln_b, sgu_w, sgu_b, q_norm_g, k_norm_g,
           out_norm_a_g, out_norm_b_g, w_out, norm2_g, w_gate, w_up, w_down):
    B, S, D = x.shape
    nb = S // BLK
    assert D == D_MODEL and S % TM1 == 0

    w_in_t = w_in.astype(_BF16).T
    cols = [None] * N_COLS
    cols[COL_LN_G], cols[COL_LN_B] = sgu_ln_g, sgu_ln_b
    cols[COL_A_G], cols[COL_B_G] = out_norm_a_g, out_norm_b_g
    cols[COL_Q_G], cols[COL_K_G] = jnp.tile(q_norm_g, N_HEADS), jnp.tile(k_norm_g, N_HEADS)
    cols = jnp.stack(cols, axis=1).astype(_F32)

    steps1 = B * (S // TM1)
    ffn_w = (w_out, w_gate, w_up, w_down)
    assert all(w.shape[0] % (16 * steps1) == 0 for w in ffn_w)
    slab_specs = [pl.BlockSpec((w.shape[0] // steps1, w.shape[1]), lambda b, s: (b * (S // TM1) + s, 0))
                  for w in ffn_w]

    ya, qT, k, vT, km, wo_b, wg_b, wup_b, wd_b = pl.pallas_call(
        _inproj_kernel,
        grid=(B, S // TM1),
        in_specs=[
            pl.BlockSpec((1, TM1, D), lambda b, s: (b, s, 0)),
            _full((1, D)),
            _full(w_in_t.shape),
            _full((WIDTH, N_COLS)),
            _full((N_HEADS, CHUNK, CHUNK)), _full((N_HEADS, CHUNK)),
            *slab_specs,
        ],
        out_specs=[
            pl.BlockSpec((1, TM1, WIDTH), lambda b, s: (b, s, 0)),
            pl.BlockSpec((1, WIDTH, TM1), lambda b, s: (b, 0, s)),
            pl.BlockSpec((1, TM1, WIDTH), lambda b, s: (b, s, 0)),
            pl.BlockSpec((1, WIDTH, TM1), lambda b, s: (b, 0, s)),
            pl.BlockSpec((1, nb, WIDTH), lambda b, s: (b, 0, 0)),
            *slab_specs,
        ],
        out_shape=[
            jax.ShapeDtypeStruct((B, S, WIDTH), _BF16),
            jax.ShapeDtypeStruct((B, WIDTH, S), _BF16),
            jax.ShapeDtypeStruct((B, S, WIDTH), _BF16),
            jax.ShapeDtypeStruct((B, WIDTH, S), _BF16),
            jax.ShapeDtypeStruct((B, nb, WIDTH), _F32),
            *[jax.ShapeDtypeStruct(w.shape, _BF16) for w in ffn_w],
        ],
        compiler_params=pltpu.CompilerParams(
            dimension_semantics=("parallel", "arbitrary"),
            vmem_limit_bytes=VMEM_LIMIT_BYTES),
        name="inproj_sgu",
    )(x, norm1_g.reshape(1, D), w_in_t, cols, sgu_w, sgu_b, *ffn_w)

    n_pair = WIDTH // PAIR
    assert S == n_pair * TM3 and B >= 2

    def attn_specs(row):
        return [
            pl.BlockSpec((1, PAIR, S), lambda r, p: (row(r), p, 0)),
            pl.BlockSpec((1, S, PAIR), lambda r, p: (row(r), 0, p)),
            pl.BlockSpec((1, PAIR, S), lambda r, p: (row(r), p, 0)),
            pl.BlockSpec((1, nb, PAIR), lambda r, p: (row(r), 0, p)),
            pl.BlockSpec((PAIR, N_COLS), lambda r, p: (p, 0)),
        ]

    kx_scratch = pltpu.VMEM((2 * nb + S, PAIR), _BF16)
    ffn_w_specs = [_full((2 * WIDTH, D)), _full((1, D)), _full((D, D_FF)), _full((D, D_FF)), _full((D_FF, D))]
    ffn_w_args = (wo_b, norm2_g.reshape(1, D), wg_b, wup_b, wd_b)

    yb0 = pl.pallas_call(
        _attn_kernel,
        grid=(1, n_pair),
        in_specs=attn_specs(lambda r: r),
        out_specs=pl.BlockSpec((1, S, PAIR), lambda r, p: (r, 0, p)),
        out_shape=jax.ShapeDtypeStruct((1, S, WIDTH), _BF16),
        scratch_shapes=[kx_scratch],
        compiler_params=pltpu.CompilerParams(
            dimension_semantics=("parallel", "parallel"),
            vmem_limit_bytes=VMEM_LIMIT_BYTES),
        name="moba_attn",
    )(qT, k, vT, km, cols)

    yb, out = pl.pallas_call(
        _attn_ffn_kernel,
        grid=(B - 1, n_pair),
        in_specs=[
            *attn_specs(lambda r: r + 1),
            pl.BlockSpec((1, TM3, D), lambda r, p: (r, p, 0)),
            pl.BlockSpec((1, TM3, WIDTH), lambda r, p: (r, p, 0)),
            _full((1, S, WIDTH)),
            *ffn_w_specs,
        ],
        out_specs=[
            pl.BlockSpec((1, S, PAIR), lambda r, p: (r, 0, p)),
            pl.BlockSpec((1, TM3, D), lambda r, p: (r, p, 0)),
        ],
        out_shape=[
            jax.ShapeDtypeStruct((B - 1, S, WIDTH), _BF16),
            jax.ShapeDtypeStruct((B, S, D), _F32),
        ],
        scratch_shapes=[kx_scratch, pltpu.VMEM((2, n_pair, S, PAIR), _BF16)],
        compiler_params=pltpu.CompilerParams(
            dimension_semantics=("arbitrary", "arbitrary"),
            vmem_limit_bytes=VMEM_LIMIT_BYTES),
        name="attn_ffn",
    )(qT, k, vT, km, cols, x, ya, yb0, *ffn_w_args)

    last = B - 1
    out = pl.pallas_call(
        _ffn_kernel,
        grid=(n_pair,),
        in_specs=[
            pl.BlockSpec((1, TM3, D), lambda i: (last, i, 0)),
            pl.BlockSpec((1, TM3, WIDTH), lambda i: (last, i, 0)),
            pl.BlockSpec((1, TM3, WIDTH), lambda i: (last - 1, i, 0)),
            *ffn_w_specs,
            pl.BlockSpec(memory_space=pl.ANY),
        ],
        out_specs=pl.BlockSpec((1, TM3, D), lambda i: (last, i, 0)),
        out_shape=jax.ShapeDtypeStruct((B, S, D), _F32),
        input_output_aliases={8: 0},
        compiler_params=pltpu.CompilerParams(
            dimension_semantics=("parallel",),
            vmem_limit_bytes=VMEM_LIMIT_BYTES),
        name="outproj_ffn",
    )(x, ya, yb, *ffn_w_args, out)
    return out


def kernel(x, norm1_g, w_in, sgu_ln_g, sgu_ln_b, sgu_w, sgu_b, q_norm_g, k_norm_g,
           out_norm_a_g, out_norm_b_g, w_out, norm2_g, w_gate, w_up, w_down):
    depth = norm1_g.shape[0]
    for l in range(depth):
        x = _layer(x, norm1_g[l], w_in[l], sgu_ln_g[l], sgu_ln_b[l], sgu_w[l], sgu_b[l],
                   q_norm_g[l], k_norm_g[l], out_norm_a_g[l], out_norm_b_g[l], w_out[l],
                   norm2_g[l], w_gate[l], w_up[l], w_down[l])
    return x
```

```python
import jax
import jax.numpy as jnp
from jax import lax
from jax.experimental import pallas as pl
from jax.experimental.pallas import tpu as pltpu

D_MODEL = 1024
HEAD_DIM = 64
N_HEADS = 8
WIDTH = N_HEADS * HEAD_DIM
CHUNK = 128
BLK = 256
TOPK = 3
D_FF = 2816
EPS = 1e-6
Q_SCALE = HEAD_DIM ** -0.5 * 1.4426950408889634

TM1 = 1024
SUB1 = 256
MIX_HEADS = 2
TM3 = 512
SUB3 = 256
HEADS_PER_STEP = 2
PAIR = HEADS_PER_STEP * HEAD_DIM
SCORES_AHEAD = 2

COL_LN_G, COL_LN_B, COL_A_G, COL_B_G, COL_Q_G, COL_K_G = range(6)
N_COLS = 6

VMEM_LIMIT_BYTES = 56 * 1024 * 1024
_NT = (((1,), (1,)), ((), ()))
_F32 = jnp.float32
_BF16 = jnp.bfloat16


def _head_rms(t, gain):
    ms = jnp.mean(t * t, axis=0, keepdims=True)
    return t * lax.rsqrt(ms + EPS) * gain


def _inproj_kernel(x_ref, g1_ref, win_ref, cols_ref, sw_ref, sb_ref,
                   wo32_ref, wg32_ref, wup32_ref, wd32_ref,
                   ya_ref, qT_ref, k_ref, vT_ref, km_ref, wo_ref, wg_ref, wup_ref, wd_ref):
    for src, dst in ((wo32_ref, wo_ref), (wg32_ref, wg_ref), (wup32_ref, wup_ref), (wd32_ref, wd_ref)):
        dst[...] = src[...].astype(_BF16)

    si = pl.program_id(1)
    nsub = TM1 // SUB1
    row = lax.broadcasted_iota(jnp.int32, (CHUNK, CHUNK), 0)
    col = lax.broadcasted_iota(jnp.int32, (CHUNK, CHUNK), 1)
    causal = col <= row
    w_mix = [jnp.where(causal, sw_ref[hd], 0.0).astype(_BF16) for hd in range(N_HEADS)]
    w_mix = [jnp.concatenate(w_mix[h0:h0 + MIX_HEADS], axis=0)
             for h0 in range(0, N_HEADS, MIX_HEADS)]

    def project(t):
        x = x_ref[0, t * SUB1:(t + 1) * SUB1, :]
        ms = jnp.mean(x * x, axis=-1, keepdims=True)
        h = (x * lax.rsqrt(ms + EPS) * g1_ref[...]).astype(_BF16)
        return [lax.dot_general(win_ref[i * WIDTH:(i + 1) * WIDTH, :], h, _NT, preferred_element_type=_F32)
                for i in range(5)]

    pending = []
    for t in range(nsub):
        pending.append((t, project(t)))
        if len(pending) > 1:
            _inproj_finish(*pending.pop(0), si * nsub, w_mix, cols_ref, sb_ref,
                           ya_ref, qT_ref, k_ref, vT_ref, km_ref)
    _inproj_finish(*pending.pop(0), si * nsub, w_mix, cols_ref, sb_ref,
                   ya_ref, qT_ref, k_ref, vT_ref, km_ref)


def _inproj_finish(t, projs, blk0, w_mix, cols_ref, sb_ref, ya_ref, qT_ref, k_ref, vT_ref, km_ref):
    pu, pv, qt, kt, vbt = projs
    nchunk = SUB1 // CHUNK
    rows = slice(t * SUB1, (t + 1) * SUB1)

    u = jax.nn.gelu(pu)
    v = jax.nn.gelu(pv)
    ya = []
    per_head = nchunk * HEAD_DIM
    for h0 in range(0, N_HEADS, MIX_HEADS):
        heads = range(h0, h0 + MIX_HEADS)
        lhs = []
        for hd in heads:
            sl = slice(hd * HEAD_DIM, (hd + 1) * HEAD_DIM)
            vh = v[sl, :]
            mu = jnp.mean(vh, axis=0, keepdims=True)
            vc = vh - mu
            var = jnp.mean(vc * vc, axis=0, keepdims=True)
            vn = (vc * lax.rsqrt(var + EPS) * cols_ref[sl, COL_LN_G:COL_LN_G + 1]
                  + cols_ref[sl, COL_LN_B:COL_LN_B + 1])
            lhs += [vn[:, c * CHUNK:(c + 1) * CHUNK] for c in range(nchunk)]
        lhs = jnp.concatenate(lhs, axis=0).astype(_BF16)
        mixed_all = lax.dot_general(lhs, w_mix[h0 // MIX_HEADS], _NT, preferred_element_type=_F32)
        for i, hd in enumerate(heads):
            sl = slice(hd * HEAD_DIM, (hd + 1) * HEAD_DIM)
            mixed = mixed_all[i * per_head:(i + 1) * per_head, i * CHUNK:(i + 1) * CHUNK]
            mixed = mixed + sb_ref[hd:hd + 1, :]
            mixed = jnp.concatenate(
                [mixed[c * HEAD_DIM:(c + 1) * HEAD_DIM, :] for c in range(nchunk)], axis=1)
            ya.append(_head_rms(u[sl, :] * mixed, cols_ref[sl, COL_A_G:COL_A_G + 1]))
    ya_ref[0, rows, :] = jnp.concatenate(ya, axis=0).T.astype(_BF16)

    qn = jnp.concatenate(
        [_head_rms(qt[hd * HEAD_DIM:(hd + 1) * HEAD_DIM, :], cols_ref[0:HEAD_DIM, COL_Q_G:COL_Q_G + 1])
         for hd in range(N_HEADS)], axis=0) * Q_SCALE
    kn = jnp.concatenate(
        [_head_rms(kt[hd * HEAD_DIM:(hd + 1) * HEAD_DIM, :], cols_ref[0:HEAD_DIM, COL_K_G:COL_K_G + 1])
         for hd in range(N_HEADS)], axis=0)
    k_tok = kn.T
    qT_ref[0, :, rows] = qn.astype(_BF16)
    vT_ref[0, :, rows] = vbt.astype(_BF16)
    k_bf = k_tok.astype(_BF16)
    for q in range(WIDTH // PAIR):
        k_ref[0, q, rows, :] = k_bf[:, q * PAIR:(q + 1) * PAIR]
    for b in range(SUB1 // BLK):
        kb = k_tok[b * BLK:(b + 1) * BLK, :]
        km_ref[0, pl.ds((blk0 + t) * (SUB1 // BLK) + b, 1), :] = jnp.mean(kb, axis=0, keepdims=True)


def _attn_jobs(qT_ref, k_ref, vT_ref, km_ref, cols_ref, kx_scr, store_tile):
    nb = k_ref.shape[2] // BLK
    frow = lax.broadcasted_iota(jnp.int32, (PAIR, BLK), 0)
    key_pos = lax.broadcasted_iota(jnp.int32, (BLK, BLK), 0)
    qry_pos = lax.broadcasted_iota(jnp.int32, (BLK, BLK), 1)
    causal_bias = jnp.where(key_pos <= qry_pos, 0.0, -jnp.inf)

    km = km_ref[0]
    km_hi = km.astype(_BF16)
    km_lo = (km - km_hi.astype(_F32)).astype(_BF16)
    kx_scr[0:2 * nb, :] = jnp.concatenate([km_hi, km_lo], axis=0)
    kx_scr[2 * nb:, :] = k_ref[0, 0]

    def scores(jj, hh):
        nkeys = (jj + 1) * BLK
        qT = qT_ref[0, :, jj * BLK:(jj + 1) * BLK]
        in_head = (frow >= hh * HEAD_DIM) & (frow < (hh + 1) * HEAD_DIM)
        qm = jnp.where(in_head, qT, jnp.zeros_like(qT))
        bias = None
        if jj > TOPK:
            sx = jnp.dot(kx_scr[0:2 * nb + nkeys, :], qm, preferred_element_type=_F32)
            gate = sx[0:jj, :] + sx[nb:nb + jj, :]
            s = sx[2 * nb:, :]
            blk_id = lax.broadcasted_iota(jnp.int32, (jj, BLK), 0)
            rank = jnp.zeros((jj, BLK), _F32)
            for m in range(jj):
                gm = gate[m:m + 1, :]
                ahead = (gm > gate) | ((gm == gate) & (blk_id > m))
                rank = rank + jnp.where(ahead, 1.0, 0.0)
            bias = jnp.where(rank < TOPK, 0.0, -jnp.inf)
        else:
            s = jnp.dot(kx_scr[2 * nb:2 * nb + nkeys, :], qm, preferred_element_type=_F32)
        blocks, m = [], None
        for n in range(jj + 1):
            sn = s[n * BLK:(n + 1) * BLK, :]
            if n == jj:
                sn = sn + causal_bias
            elif bias is not None:
                sn = sn + bias[n:n + 1, :]
            blocks.append(sn)
            mn = jnp.max(sn, axis=0, keepdims=True)
            m = mn if m is None else jnp.maximum(m, mn)
        return blocks, m

    tiles = {}

    def softmax(jj, hh, blocks, m):
        probs, l = [], None
        for sn in blocks:
            p = jnp.exp2(sn - m)
            ln = jnp.sum(p, axis=0, keepdims=True)
            l = ln if l is None else l + ln
            probs.append(p.astype(_BF16))
        p = jnp.concatenate(probs, axis=0) if jj > 0 else probs[0]
        return jj, hh, p, l

    def pv(jj, hh, p, l):
        vrows = slice(hh * HEAD_DIM, (hh + 1) * HEAD_DIM)
        o = jnp.dot(vT_ref[0, vrows, 0:(jj + 1) * BLK], p, preferred_element_type=_F32)
        tiles.setdefault(jj, []).append(
            _head_rms(o / l, cols_ref[vrows, COL_B_G:COL_B_G + 1]))
        if len(tiles[jj]) == HEADS_PER_STEP:
            y = jnp.concatenate(tiles.pop(jj), axis=0)
            store_tile(jj, y.T.astype(_BF16))

    jobs = [(jj, hh) for jj in range(nb) for hh in range(HEADS_PER_STEP)]
    return jobs, scores, softmax, pv


def _attn_kernel(qT_ref, k_ref, vT_ref, km_ref, cols_ref, yb_ref, kx_scr):
    def store_tile(jj, tile):
        yb_ref[0, 0, jj * BLK:(jj + 1) * BLK, :] = tile

    jobs, scores, softmax, pv = _attn_jobs(qT_ref, k_ref, vT_ref, km_ref, cols_ref, kx_scr, store_tile)
    pending = []
    for job in jobs:
        pending.append((*job, *scores(*job)))
        if len(pending) > SCORES_AHEAD:
            pv(*softmax(*pending.pop(0)))
    for job in pending:
        pv(*softmax(*job))


def _ffn_stages(load_x, load_y, wo_ref, g2_ref, wg_ref, wup_ref, wd_ref, store):
    st = {}

    def outproj():
        st["x1"] = load_x() + jnp.dot(load_y(), wo_ref[...], preferred_element_type=_F32)

    def norm():
        x1 = st["x1"]
        ms = jnp.mean(x1 * x1, axis=-1, keepdims=True)
        st["h"] = (x1 * lax.rsqrt(ms + EPS) * g2_ref[...]).astype(_BF16)

    def gate():
        st["g"] = jnp.dot(st["h"], wg_ref[...], preferred_element_type=_F32)

    def up():
        st["u"] = jnp.dot(st.pop("h"), wup_ref[...], preferred_element_type=_F32)

    def act():
        st["a"] = (jax.nn.silu(st.pop("g")) * st.pop("u")).astype(_BF16)

    def down():
        store(st.pop("x1") + jnp.dot(st.pop("a"), wd_ref[...], preferred_element_type=_F32))

    return outproj, norm, gate, up, act, down


def _ffn_kernel(x_ref, ya_ref, yb_ref, wo_ref, g2_ref, wg_ref, wup_ref, wd_ref, _aliased_out, o_ref):
    def store(val):
        o_ref[0] = val

    for stage in _ffn_stages(lambda: x_ref[0],
                             lambda: jnp.concatenate(
                                 [ya_ref[0]] + [yb_ref[0, q] for q in range(yb_ref.shape[1])], axis=1),
                             wo_ref, g2_ref, wg_ref, wup_ref, wd_ref, store):
        stage()


def _attn_ffn_kernel(qT_ref, k_ref, vT_ref, km_ref, cols_ref,
                     x_ref, ya_ref, yb0_ref, wo_ref, g2_ref, wg_ref, wup_ref, wd_ref,
                     yb_ref, o_ref, kx_scr, yb_scr):
    r, p = pl.program_id(0), pl.program_id(1)
    n_pair = yb_scr.shape[1]

    @pl.when((r == 0) & (p == 0))
    def _():
        for q in range(n_pair):
            yb_scr[0, q] = yb0_ref[0, q]

    rd = r % 2
    wr = 1 - rd

    def store_tile(jj, tile):
        yb_ref[0, 0, jj * BLK:(jj + 1) * BLK, :] = tile
        yb_scr[wr, p, jj * BLK:(jj + 1) * BLK, :] = tile

    jobs, scores, softmax, pv = _attn_jobs(qT_ref, k_ref, vT_ref, km_ref, cols_ref, kx_scr, store_tile)

    subs = []
    for t in range(TM3 // SUB3):
        rows = slice(t * SUB3, (t + 1) * SUB3)

        def load_y(t=t, rows=rows):
            tok = pl.ds(pl.multiple_of(p * TM3 + t * SUB3, SUB3), SUB3)
            return jnp.concatenate(
                [ya_ref[0, rows, :]] + [yb_scr[rd, q, tok, :] for q in range(n_pair)], axis=1)

        def store(val, rows=rows):
            o_ref[0, rows, :] = val

        subs.append(_ffn_stages(lambda rows=rows: x_ref[0, rows, :], load_y,
                                wo_ref, g2_ref, wg_ref, wup_ref, wd_ref, store))
    (a_out, a_norm, a_gate, a_up, a_act, a_down), (b_out, b_norm, b_gate, b_up, b_act, b_down) = subs
    slots = [(a_out,), (b_out, a_norm), (a_gate, b_norm), (a_up, a_act),
             (b_gate,), (b_up, b_act), (a_down,), (b_down,)]
    per_slot = len(jobs) // len(slots)
    assert per_slot * len(slots) == len(jobs)
    for i, slot in enumerate(slots):
        mine = jobs[i * per_slot:(i + 1) * per_slot]
        started = [(*job, *scores(*job)) for job in mine]
        weighted = [softmax(*job) for job in started]
        for stage in slot:
            stage()
        for job in weighted:
            pv(*job)


def _full(shape):
    return pl.BlockSpec(shape, lambda *_: (0,) * len(shape), pipeline_mode=pl.Buffered(1))


def _layer(x, norm1_g, w_in, sgu_ln_g, sgu_ln_b, sgu_w, sgu_b, q_norm_g, k_norm_g,
           out_norm_a_g, out_norm_b_g, w_out, norm2_g, w_gate, w_up, w_down):
    B, S, D = x.shape
    nb = S // BLK
    assert D == D_MODEL and S % TM1 == 0

    w_in_t = w_in.astype(_BF16).T
    cols = [None] * N_COLS
    cols[COL_LN_G], cols[COL_LN_B] = sgu_ln_g, sgu_ln_b
    cols[COL_A_G], cols[COL_B_G] = out_norm_a_g, out_norm_b_g
    cols[COL_Q_G], cols[COL_K_G] = jnp.tile(q_norm_g, N_HEADS), jnp.tile(k_norm_g, N_HEADS)
    cols = jnp.stack(cols, axis=1).astype(_F32)

    steps1 = B * (S // TM1)
    ffn_w = (w_out, w_gate, w_up, w_down)
    assert all(w.shape[0] % (16 * steps1) == 0 for w in ffn_w)
    slab_specs = [pl.BlockSpec((w.shape[0] // steps1, w.shape[1]), lambda b, s: (b * (S // TM1) + s, 0))
                  for w in ffn_w]

    ya, qT, k, vT, km, wo_b, wg_b, wup_b, wd_b = pl.pallas_call(
        _inproj_kernel,
        grid=(B, S // TM1),
        in_specs=[
            pl.BlockSpec((1, TM1, D), lambda b, s: (b, s, 0)),
            _full((1, D)),
            _full(w_in_t.shape),
            _full((WIDTH, N_COLS)),
            _full((N_HEADS, CHUNK, CHUNK)), _full((N_HEADS, CHUNK)),
            *slab_specs,
        ],
        out_specs=[
            pl.BlockSpec((1, TM1, WIDTH), lambda b, s: (b, s, 0)),
            pl.BlockSpec((1, WIDTH, TM1), lambda b, s: (b, 0, s)),
            pl.BlockSpec((1, WIDTH // PAIR, TM1, PAIR), lambda b, s: (b, 0, s, 0)),
            pl.BlockSpec((1, WIDTH, TM1), lambda b, s: (b, 0, s)),
            pl.BlockSpec((1, nb, WIDTH), lambda b, s: (b, 0, 0)),
            *slab_specs,
        ],
        out_shape=[
            jax.ShapeDtypeStruct((B, S, WIDTH), _BF16),
            jax.ShapeDtypeStruct((B, WIDTH, S), _BF16),
            jax.ShapeDtypeStruct((B, WIDTH // PAIR, S, PAIR), _BF16),
            jax.ShapeDtypeStruct((B, WIDTH, S), _BF16),
            jax.ShapeDtypeStruct((B, nb, WIDTH), _F32),
            *[jax.ShapeDtypeStruct(w.shape, _BF16) for w in ffn_w],
        ],
        compiler_params=pltpu.CompilerParams(
            dimension_semantics=("parallel", "arbitrary"),
            vmem_limit_bytes=VMEM_LIMIT_BYTES),
        name="inproj_sgu",
    )(x, norm1_g.reshape(1, D), w_in_t, cols, sgu_w, sgu_b, *ffn_w)

    n_pair = WIDTH // PAIR
    assert S == n_pair * TM3 and B >= 2

    def attn_specs(row):
        return [
            pl.BlockSpec((1, PAIR, S), lambda r, p: (row(r), p, 0)),
            pl.BlockSpec((1, 1, S, PAIR), lambda r, p: (row(r), p, 0, 0)),
            pl.BlockSpec((1, PAIR, S), lambda r, p: (row(r), p, 0)),
            pl.BlockSpec((1, nb, PAIR), lambda r, p: (row(r), 0, p)),
            pl.BlockSpec((PAIR, N_COLS), lambda r, p: (p, 0)),
        ]

    kx_scratch = pltpu.VMEM((2 * nb + S, PAIR), _BF16)
    ffn_w_specs = [_full((2 * WIDTH, D)), _full((1, D)), _full((D, D_FF)), _full((D, D_FF)), _full((D_FF, D))]
    ffn_w_args = (wo_b, norm2_g.reshape(1, D), wg_b, wup_b, wd_b)

    yb0 = pl.pallas_call(
        _attn_kernel,
        grid=(1, n_pair),
        in_specs=attn_specs(lambda r: r),
        out_specs=pl.BlockSpec((1, 1, S, PAIR), lambda r, p: (r, p, 0, 0)),
        out_shape=jax.ShapeDtypeStruct((1, n_pair, S, PAIR), _BF16),
        scratch_shapes=[kx_scratch],
        compiler_params=pltpu.CompilerParams(
            dimension_semantics=("parallel", "parallel"),
            vmem_limit_bytes=VMEM_LIMIT_BYTES),
        name="moba_attn",
    )(qT, k, vT, km, cols)

    yb, out = pl.pallas_call(
        _attn_ffn_kernel,
        grid=(B - 1, n_pair),
        in_specs=[
            *attn_specs(lambda r: r + 1),
            pl.BlockSpec((1, TM3, D), lambda r, p: (r, p, 0)),
            pl.BlockSpec((1, TM3, WIDTH), lambda r, p: (r, p, 0)),
            _full((1, n_pair, S, PAIR)),
            *ffn_w_specs,
        ],
        out_specs=[
            pl.BlockSpec((1, 1, S, PAIR), lambda r, p: (r, p, 0, 0)),
            pl.BlockSpec((1, TM3, D), lambda r, p: (r, p, 0)),
        ],
        out_shape=[
            jax.ShapeDtypeStruct((B - 1, n_pair, S, PAIR), _BF16),
            jax.ShapeDtypeStruct((B, S, D), _F32),
        ],
        scratch_shapes=[kx_scratch, pltpu.VMEM((2, n_pair, S, PAIR), _BF16)],
        compiler_params=pltpu.CompilerParams(
            dimension_semantics=("arbitrary", "arbitrary"),
            vmem_limit_bytes=VMEM_LIMIT_BYTES),
        name="attn_ffn",
    )(qT, k, vT, km, cols, x, ya, yb0, *ffn_w_args)

    last = B - 1
    out = pl.pallas_call(
        _ffn_kernel,
        grid=(n_pair,),
        in_specs=[
            pl.BlockSpec((1, TM3, D), lambda i: (last, i, 0)),
            pl.BlockSpec((1, TM3, WIDTH), lambda i: (last, i, 0)),
            pl.BlockSpec((1, n_pair, TM3, PAIR), lambda i: (last - 1, 0, i, 0)),
            *ffn_w_specs,
            pl.BlockSpec(memory_space=pl.ANY),
        ],
        out_specs=pl.BlockSpec((1, TM3, D), lambda i: (last, i, 0)),
        out_shape=jax.ShapeDtypeStruct((B, S, D), _F32),
        input_output_aliases={8: 0},
        compiler_params=pltpu.CompilerParams(
            dimension_semantics=("parallel",),
            vmem_limit_bytes=VMEM_LIMIT_BYTES),
        name="outproj_ffn",
    )(x, ya, yb, *ffn_w_args, out)
    return out


def kernel(x, norm1_g, w_in, sgu_ln_g, sgu_ln_b, sgu_w, sgu_b, q_norm_g, k_norm_g,
           out_norm_a_g, out_norm_b_g, w_out, norm2_g, w_gate, w_up, w_down):
    depth = norm1_g.shape[0]
    for l in range(depth):
        x = _layer(x, norm1_g[l], w_in[l], sgu_ln_g[l], sgu_ln_b[l], sgu_w[l], sgu_b[l],
                   q_norm_g[l], k_norm_g[l], out_norm_a_g[l], out_norm_b_g[l], w_out[l],
                   norm2_g[l], w_gate[l], w_up[l], w_down[l])
    return x
```

```python
import jax
import jax.numpy as jnp
from jax import lax
from jax.experimental import pallas as pl
from jax.experimental.pallas import tpu as pltpu

D_MODEL = 1024
HEAD_DIM = 64
N_HEADS = 8
WIDTH = N_HEADS * HEAD_DIM
CHUNK = 128
BLK = 256
TOPK = 3
D_FF = 2816
EPS = 1e-6
Q_SCALE = HEAD_DIM ** -0.5 * 1.4426950408889634

TM1 = 1024
SUB1 = 256
MIX_HEADS = 2
TM3 = 512
SUB3 = 256
HEADS_PER_STEP = 2
PAIR = HEADS_PER_STEP * HEAD_DIM
SCORES_AHEAD = 2

COL_LN_G, COL_LN_B, COL_A_G, COL_B_G, COL_Q_G, COL_K_G = range(6)
N_COLS = 6

VMEM_LIMIT_MIB = {"inproj_sgu": 40, "moba_attn": 16, "attn_ffn": 56, "outproj_ffn": 44}
_NT = (((1,), (1,)), ((), ()))
_F32 = jnp.float32
_BF16 = jnp.bfloat16


def _head_rms(t, gain):
    ms = jnp.mean(t * t, axis=0, keepdims=True)
    return t * lax.rsqrt(ms + EPS) * gain


def _inproj_kernel(x_ref, g1_ref, win_ref, cols_ref, sw_ref, sb_ref,
                   wo32_ref, wg32_ref, wup32_ref, wd32_ref,
                   ya_ref, qT_ref, k_ref, vT_ref, km_ref, wo_ref, wg_ref, wup_ref, wd_ref):
    for src, dst in ((wo32_ref, wo_ref), (wg32_ref, wg_ref), (wup32_ref, wup_ref), (wd32_ref, wd_ref)):
        dst[...] = src[...].astype(_BF16)

    si = pl.program_id(1)
    nsub = TM1 // SUB1
    row = lax.broadcasted_iota(jnp.int32, (CHUNK, CHUNK), 0)
    col = lax.broadcasted_iota(jnp.int32, (CHUNK, CHUNK), 1)
    causal = col <= row
    w_mix = [jnp.where(causal, sw_ref[hd], 0.0).astype(_BF16) for hd in range(N_HEADS)]
    w_mix = [jnp.concatenate(w_mix[h0:h0 + MIX_HEADS], axis=0)
             for h0 in range(0, N_HEADS, MIX_HEADS)]

    def project(t):
        x = x_ref[0, t * SUB1:(t + 1) * SUB1, :]
        ms = jnp.mean(x * x, axis=-1, keepdims=True)
        h = (x * lax.rsqrt(ms + EPS) * g1_ref[...]).astype(_BF16)
        return [lax.dot_general(win_ref[i * WIDTH:(i + 1) * WIDTH, :], h, _NT, preferred_element_type=_F32)
                for i in range(5)]

    pending = []
    for t in range(nsub):
        pending.append((t, project(t)))
        if len(pending) > 1:
            _inproj_finish(*pending.pop(0), si * nsub, w_mix, cols_ref, sb_ref,
                           ya_ref, qT_ref, k_ref, vT_ref, km_ref)
    _inproj_finish(*pending.pop(0), si * nsub, w_mix, cols_ref, sb_ref,
                   ya_ref, qT_ref, k_ref, vT_ref, km_ref)


def _inproj_finish(t, projs, blk0, w_mix, cols_ref, sb_ref, ya_ref, qT_ref, k_ref, vT_ref, km_ref):
    pu, pv, qt, kt, vbt = projs
    nchunk = SUB1 // CHUNK
    rows = slice(t * SUB1, (t + 1) * SUB1)

    u = jax.nn.gelu(pu)
    v = jax.nn.gelu(pv)
    ya = []
    per_head = nchunk * HEAD_DIM
    for h0 in range(0, N_HEADS, MIX_HEADS):
        heads = range(h0, h0 + MIX_HEADS)
        lhs = []
        for hd in heads:
            sl = slice(hd * HEAD_DIM, (hd + 1) * HEAD_DIM)
            vh = v[sl, :]
            mu = jnp.mean(vh, axis=0, keepdims=True)
            vc = vh - mu
            var = jnp.mean(vc * vc, axis=0, keepdims=True)
            vn = (vc * lax.rsqrt(var + EPS) * cols_ref[sl, COL_LN_G:COL_LN_G + 1]
                  + cols_ref[sl, COL_LN_B:COL_LN_B + 1])
            lhs += [vn[:, c * CHUNK:(c + 1) * CHUNK] for c in range(nchunk)]
        lhs = jnp.concatenate(lhs, axis=0).astype(_BF16)
        mixed_all = lax.dot_general(lhs, w_mix[h0 // MIX_HEADS], _NT, preferred_element_type=_F32)
        for i, hd in enumerate(heads):
            sl = slice(hd * HEAD_DIM, (hd + 1) * HEAD_DIM)
            mixed = mixed_all[i * per_head:(i + 1) * per_head, i * CHUNK:(i + 1) * CHUNK]
            mixed = mixed + sb_ref[hd:hd + 1, :]
            mixed = jnp.concatenate(
                [mixed[c * HEAD_DIM:(c + 1) * HEAD_DIM, :] for c in range(nchunk)], axis=1)
            ya.append(_head_rms(u[sl, :] * mixed, cols_ref[sl, COL_A_G:COL_A_G + 1]))
    ya_ref[0, rows, :] = jnp.concatenate(ya, axis=0).T.astype(_BF16)

    qn = jnp.concatenate(
        [_head_rms(qt[hd * HEAD_DIM:(hd + 1) * HEAD_DIM, :], cols_ref[0:HEAD_DIM, COL_Q_G:COL_Q_G + 1])
         for hd in range(N_HEADS)], axis=0) * Q_SCALE
    kn = jnp.concatenate(
        [_head_rms(kt[hd * HEAD_DIM:(hd + 1) * HEAD_DIM, :], cols_ref[0:HEAD_DIM, COL_K_G:COL_K_G + 1])
         for hd in range(N_HEADS)], axis=0)
    k_tok = kn.T
    qT_ref[0, :, rows] = qn.astype(_BF16)
    vT_ref[0, :, rows] = vbt.astype(_BF16)
    k_ref[0, rows, :] = k_tok.astype(_BF16)
    for b in range(SUB1 // BLK):
        kb = k_tok[b * BLK:(b + 1) * BLK, :]
        km_ref[0, pl.ds((blk0 + t) * (SUB1 // BLK) + b, 1), :] = jnp.mean(kb, axis=0, keepdims=True)


def _attn_jobs(qT_ref, k_ref, vT_ref, km_ref, cols_ref, kx_scr, store_tile):
    nb = k_ref.shape[1] // BLK
    frow = lax.broadcasted_iota(jnp.int32, (PAIR, BLK), 0)
    key_pos = lax.broadcasted_iota(jnp.int32, (BLK, BLK), 0)
    qry_pos = lax.broadcasted_iota(jnp.int32, (BLK, BLK), 1)
    causal_bias = jnp.where(key_pos <= qry_pos, 0.0, -jnp.inf)

    km = km_ref[0]
    km_hi = km.astype(_BF16)
    km_lo = (km - km_hi.astype(_F32)).astype(_BF16)
    kx_scr[0:2 * nb, :] = jnp.concatenate([km_hi, km_lo], axis=0)
    kx_scr[2 * nb:, :] = k_ref[0]

    def scores(jj, hh):
        nkeys = (jj + 1) * BLK
        qT = qT_ref[0, :, jj * BLK:(jj + 1) * BLK]
        in_head = (frow >= hh * HEAD_DIM) & (frow < (hh + 1) * HEAD_DIM)
        qm = jnp.where(in_head, qT, jnp.zeros_like(qT))
        bias = None
        if jj > TOPK:
            sx = jnp.dot(kx_scr[0:2 * nb + nkeys, :], qm, preferred_element_type=_F32)
            gate = sx[0:jj, :] + sx[nb:nb + jj, :]
            s = sx[2 * nb:, :]
            blk_id = lax.broadcasted_iota(jnp.int32, (jj, BLK), 0)
            rank = jnp.zeros((jj, BLK), _F32)
            for m in range(jj):
                gm = gate[m:m + 1, :]
                ahead = (gm > gate) | ((gm == gate) & (blk_id > m))
                rank = rank + jnp.where(ahead, 1.0, 0.0)
            bias = jnp.where(rank < TOPK, 0.0, -jnp.inf)
        else:
            s = jnp.dot(kx_scr[2 * nb:2 * nb + nkeys, :], qm, preferred_element_type=_F32)
        blocks, m = [], None
        for n in range(jj + 1):
            sn = s[n * BLK:(n + 1) * BLK, :]
            if n == jj:
                sn = sn + causal_bias
            elif bias is not None:
                sn = sn + bias[n:n + 1, :]
            blocks.append(sn)
            mn = jnp.max(sn, axis=0, keepdims=True)
            m = mn if m is None else jnp.maximum(m, mn)
        return blocks, m

    tiles = {}

    def softmax(jj, hh, blocks, m):
        probs, l = [], None
        for sn in blocks:
            p = jnp.exp2(sn - m)
            ln = jnp.sum(p, axis=0, keepdims=True)
            l = ln if l is None else l + ln
            probs.append(p.astype(_BF16))
        p = jnp.concatenate(probs, axis=0) if jj > 0 else probs[0]
        return jj, hh, p, l

    def pv(jj, hh, p, l):
        vrows = slice(hh * HEAD_DIM, (hh + 1) * HEAD_DIM)
        o = jnp.dot(vT_ref[0, vrows, 0:(jj + 1) * BLK], p, preferred_element_type=_F32)
        tiles.setdefault(jj, []).append(
            _head_rms(o / l, cols_ref[vrows, COL_B_G:COL_B_G + 1]))
        if len(tiles[jj]) == HEADS_PER_STEP:
            y = jnp.concatenate(tiles.pop(jj), axis=0)
            store_tile(jj, y.T.astype(_BF16))

    jobs = [(jj, hh) for jj in range(nb) for hh in range(HEADS_PER_STEP)]
    return jobs, scores, softmax, pv


def _attn_kernel(qT_ref, k_ref, vT_ref, km_ref, cols_ref, yb_ref, kx_scr):
    def store_tile(jj, tile):
        yb_ref[0, jj * BLK:(jj + 1) * BLK, :] = tile

    jobs, scores, softmax, pv = _attn_jobs(qT_ref, k_ref, vT_ref, km_ref, cols_ref, kx_scr, store_tile)
    pending = []
    for job in jobs:
        pending.append((*job, *scores(*job)))
        if len(pending) > SCORES_AHEAD:
            pv(*softmax(*pending.pop(0)))
    for job in pending:
        pv(*softmax(*job))


def _ffn_stages(load_x, load_y, wo_ref, g2_ref, wg_ref, wup_ref, wd_ref, store):
    st = {}

    def outproj():
        st["x1"] = load_x() + jnp.dot(load_y(), wo_ref[...], preferred_element_type=_F32)

    def norm():
        x1 = st["x1"]
        ms = jnp.mean(x1 * x1, axis=-1, keepdims=True)
        st["h"] = (x1 * lax.rsqrt(ms + EPS) * g2_ref[...]).astype(_BF16)

    def gate():
        st["g"] = jnp.dot(st["h"], wg_ref[...], preferred_element_type=_F32)

    def up():
        st["u"] = jnp.dot(st.pop("h"), wup_ref[...], preferred_element_type=_F32)

    def act():
        st["a"] = (jax.nn.silu(st.pop("g")) * st.pop("u")).astype(_BF16)

    def down():
        store(st.pop("x1") + jnp.dot(st.pop("a"), wd_ref[...], preferred_element_type=_F32))

    return outproj, norm, gate, up, act, down


def _ffn_kernel(x_ref, ya_ref, yb_ref, wo_ref, g2_ref, wg_ref, wup_ref, wd_ref, _aliased_out, o_ref):
    def store(val):
        o_ref[0] = val

    for stage in _ffn_stages(lambda: x_ref[0],
                             lambda: jnp.concatenate([ya_ref[0], yb_ref[0]], axis=1),
                             wo_ref, g2_ref, wg_ref, wup_ref, wd_ref, store):
        stage()


def _attn_ffn_kernel(qT_ref, k_ref, vT_ref, km_ref, cols_ref,
                     x_ref, ya_ref, yb0_ref, wo_ref, g2_ref, wg_ref, wup_ref, wd_ref,
                     yb_ref, o_ref, kx_scr, yb_scr):
    r, p = pl.program_id(0), pl.program_id(1)
    n_pair = yb_scr.shape[1]

    @pl.when((r == 0) & (p == 0))
    def _():
        for q in range(n_pair):
            yb_scr[0, q] = yb0_ref[0, :, q * PAIR:(q + 1) * PAIR]

    rd = r % 2
    wr = 1 - rd

    def store_tile(jj, tile):
        yb_ref[0, jj * BLK:(jj + 1) * BLK, :] = tile
        yb_scr[wr, p, jj * BLK:(jj + 1) * BLK, :] = tile

    jobs, scores, softmax, pv = _attn_jobs(qT_ref, k_ref, vT_ref, km_ref, cols_ref, kx_scr, store_tile)

    subs = []
    for t in range(TM3 // SUB3):
        rows = slice(t * SUB3, (t + 1) * SUB3)

        def load_y(t=t, rows=rows):
            tok = pl.ds(pl.multiple_of(p * TM3 + t * SUB3, SUB3), SUB3)
            return jnp.concatenate(
                [ya_ref[0, rows, :]] + [yb_scr[rd, q, tok, :] for q in range(n_pair)], axis=1)

        def store(val, rows=rows):
            o_ref[0, rows, :] = val

        subs.append(_ffn_stages(lambda rows=rows: x_ref[0, rows, :], load_y,
                                wo_ref, g2_ref, wg_ref, wup_ref, wd_ref, store))
    (a_out, a_norm, a_gate, a_up, a_act, a_down), (b_out, b_norm, b_gate, b_up, b_act, b_down) = subs
    slots = [(a_out,), (b_out, a_norm), (a_gate, b_norm), (a_up, a_act),
             (b_gate,), (b_up, b_act), (a_down,), (b_down,)]
    per_slot = len(jobs) // len(slots)
    assert per_slot * len(slots) == len(jobs)
    for i, slot in enumerate(slots):
        mine = jobs[i * per_slot:(i + 1) * per_slot]
        started = [(*job, *scores(*job)) for job in mine]
        weighted = [softmax(*job) for job in started]
        for stage in slot:
            stage()
        for job in weighted:
            pv(*job)


def _vmem_limit(call_name):
    return VMEM_LIMIT_MIB[call_name] * 1024 * 1024


def _full(shape):
    return pl.BlockSpec(shape, lambda *_: (0,) * len(shape), pipeline_mode=pl.Buffered(1))


def _layer(x, norm1_g, w_in, sgu_ln_g, sgu_ln_b, sgu_w, sgu_b, q_norm_g, k_norm_g,
           out_norm_a_g, out_norm_b_g, w_out, norm2_g, w_gate, w_up, w_down):
    B, S, D = x.shape
    nb = S // BLK
    assert D == D_MODEL and S % TM1 == 0

    w_in_t = w_in.astype(_BF16).T
    cols = [None] * N_COLS
    cols[COL_LN_G], cols[COL_LN_B] = sgu_ln_g, sgu_ln_b
    cols[COL_A_G], cols[COL_B_G] = out_norm_a_g, out_norm_b_g
    cols[COL_Q_G], cols[COL_K_G] = jnp.tile(q_norm_g, N_HEADS), jnp.tile(k_norm_g, N_HEADS)
    cols = jnp.stack(cols, axis=1).astype(_F32)

    steps1 = B * (S // TM1)
    ffn_w = (w_out, w_gate, w_up, w_down)
    assert all(w.shape[0] % (16 * steps1) == 0 for w in ffn_w)
    slab_specs = [pl.BlockSpec((w.shape[0] // steps1, w.shape[1]), lambda b, s: (b * (S // TM1) + s, 0))
                  for w in ffn_w]

    ya, qT, k, vT, km, wo_b, wg_b, wup_b, wd_b = pl.pallas_call(
        _inproj_kernel,
        grid=(B, S // TM1),
        in_specs=[
            pl.BlockSpec((1, TM1, D), lambda b, s: (b, s, 0)),
            _full((1, D)),
            _full(w_in_t.shape),
            _full((WIDTH, N_COLS)),
            _full((N_HEADS, CHUNK, CHUNK)), _full((N_HEADS, CHUNK)),
            *slab_specs,
        ],
        out_specs=[
            pl.BlockSpec((1, TM1, WIDTH), lambda b, s: (b, s, 0)),
            pl.BlockSpec((1, WIDTH, TM1), lambda b, s: (b, 0, s)),
            pl.BlockSpec((1, TM1, WIDTH), lambda b, s: (b, s, 0)),
            pl.BlockSpec((1, WIDTH, TM1), lambda b, s: (b, 0, s)),
            pl.BlockSpec((1, nb, WIDTH), lambda b, s: (b, 0, 0)),
            *slab_specs,
        ],
        out_shape=[
            jax.ShapeDtypeStruct((B, S, WIDTH), _BF16),
            jax.ShapeDtypeStruct((B, WIDTH, S), _BF16),
            jax.ShapeDtypeStruct((B, S, WIDTH), _BF16),
            jax.ShapeDtypeStruct((B, WIDTH, S), _BF16),
            jax.ShapeDtypeStruct((B, nb, WIDTH), _F32),
            *[jax.ShapeDtypeStruct(w.shape, _BF16) for w in ffn_w],
        ],
        compiler_params=pltpu.CompilerParams(
            dimension_semantics=("parallel", "arbitrary"),
            vmem_limit_bytes=_vmem_limit("inproj_sgu")),
        name="inproj_sgu",
    )(x, norm1_g.reshape(1, D), w_in_t, cols, sgu_w, sgu_b, *ffn_w)

    n_pair = WIDTH // PAIR
    assert S == n_pair * TM3 and B >= 2

    def attn_specs(row):
        return [
            pl.BlockSpec((1, PAIR, S), lambda r, p: (row(r), p, 0)),
            pl.BlockSpec((1, S, PAIR), lambda r, p: (row(r), 0, p)),
            pl.BlockSpec((1, PAIR, S), lambda r, p: (row(r), p, 0)),
            pl.BlockSpec((1, nb, PAIR), lambda r, p: (row(r), 0, p)),
            pl.BlockSpec((PAIR, N_COLS), lambda r, p: (p, 0)),
        ]

    kx_scratch = pltpu.VMEM((2 * nb + S, PAIR), _BF16)
    ffn_w_specs = [_full((2 * WIDTH, D)), _full((1, D)), _full((D, D_FF)), _full((D, D_FF)), _full((D_FF, D))]
    ffn_w_args = (wo_b, norm2_g.reshape(1, D), wg_b, wup_b, wd_b)

    yb0 = pl.pallas_call(
        _attn_kernel,
        grid=(1, n_pair),
        in_specs=attn_specs(lambda r: r),
        out_specs=pl.BlockSpec((1, S, PAIR), lambda r, p: (r, 0, p)),
        out_shape=jax.ShapeDtypeStruct((1, S, WIDTH), _BF16),
        scratch_shapes=[kx_scratch],
        compiler_params=pltpu.CompilerParams(
            dimension_semantics=("parallel", "parallel"),
            vmem_limit_bytes=_vmem_limit("moba_attn")),
        name="moba_attn",
    )(qT, k, vT, km, cols)

    yb, out = pl.pallas_call(
        _attn_ffn_kernel,
        grid=(B - 1, n_pair),
        in_specs=[
            *attn_specs(lambda r: r + 1),
            pl.BlockSpec((1, TM3, D), lambda r, p: (r, p, 0)),
            pl.BlockSpec((1, TM3, WIDTH), lambda r, p: (r, p, 0)),
            _full((1, S, WIDTH)),
            *ffn_w_specs,
        ],
        out_specs=[
            pl.BlockSpec((1, S, PAIR), lambda r, p: (r, 0, p)),
            pl.BlockSpec((1, TM3, D), lambda r, p: (r, p, 0)),
        ],
        out_shape=[
            jax.ShapeDtypeStruct((B - 1, S, WIDTH), _BF16),
            jax.ShapeDtypeStruct((B, S, D), _F32),
        ],
        scratch_shapes=[kx_scratch, pltpu.VMEM((2, n_pair, S, PAIR), _BF16)],
        compiler_params=pltpu.CompilerParams(
            dimension_semantics=("arbitrary", "arbitrary"),
            vmem_limit_bytes=_vmem_limit("attn_ffn")),
        name="attn_ffn",
    )(qT, k, vT, km, cols, x, ya, yb0, *ffn_w_args)

    last = B - 1
    out = pl.pallas_call(
        _ffn_kernel,
        grid=(n_pair,),
        in_specs=[
            pl.BlockSpec((1, TM3, D), lambda i: (last, i, 0)),
            pl.BlockSpec((1, TM3, WIDTH), lambda i: (last, i, 0)),
            pl.BlockSpec((1, TM3, WIDTH), lambda i: (last - 1, i, 0)),
            *ffn_w_specs,
            pl.BlockSpec(memory_space=pl.ANY),
        ],
        out_specs=pl.BlockSpec((1, TM3, D), lambda i: (last, i, 0)),
        out_shape=jax.ShapeDtypeStruct((B, S, D), _F32),
        input_output_aliases={8: 0},
        compiler_params=pltpu.CompilerParams(
            dimension_semantics=("parallel",),
            vmem_limit_bytes=_vmem_limit("outproj_ffn")),
        name="outproj_ffn",
    )(x, ya, yb, *ffn_w_args, out)
    return out


def kernel(x, norm1_g, w_in, sgu_ln_g, sgu_ln_b, sgu_w, sgu_b, q_norm_g, k_norm_g,
           out_norm_a_g, out_norm_b_g, w_out, norm2_g, w_gate, w_up, w_down):
    depth = norm1_g.shape[0]
    for l in range(depth):
        x = _layer(x, norm1_g[l], w_in[l], sgu_ln_g[l], sgu_ln_b[l], sgu_w[l], sgu_b[l],
                   q_norm_g[l], k_norm_g[l], out_norm_a_g[l], out_norm_b_g[l], w_out[l],
                   norm2_g[l], w_gate[l], w_up[l], w_down[l])
    return x
```

```python
import jax
import jax.numpy as jnp
from jax import lax
from jax.experimental import pallas as pl
from jax.experimental.pallas import tpu as pltpu

D_MODEL = 1024
HEAD_DIM = 64
N_HEADS = 8
WIDTH = N_HEADS * HEAD_DIM
CHUNK = 128
BLK = 256
TOPK = 3
D_FF = 2816
EPS = 1e-6
Q_SCALE = HEAD_DIM ** -0.5 * 1.4426950408889634

TM1 = 1024
SUB1 = 256
MIX_HEADS = 2
TM3 = 512
SUB3 = 256
HEADS_PER_STEP = 2
PAIR = HEADS_PER_STEP * HEAD_DIM
SCORES_AHEAD = 2

COL_LN_G, COL_LN_B, COL_A_G, COL_B_G, COL_Q_G, COL_K_G = range(6)
N_COLS = 6

VMEM_LIMIT_BYTES = 56 * 1024 * 1024
_NT = (((1,), (1,)), ((), ()))
_F32 = jnp.float32
_BF16 = jnp.bfloat16


def _head_rms(t, gain):
    ms = jnp.mean(t * t, axis=0, keepdims=True)
    return t * lax.rsqrt(ms + EPS) * gain


def _inproj_kernel(x_ref, g1_ref, win_ref, cols_ref, sw_ref, sb_ref,
                   wo32_ref, wg32_ref, wup32_ref, wd32_ref,
                   ya_ref, qT_ref, k_ref, vT_ref, km_ref, wo_ref, wg_ref, wup_ref, wd_ref):
    for src, dst in ((wo32_ref, wo_ref), (wg32_ref, wg_ref), (wup32_ref, wup_ref), (wd32_ref, wd_ref)):
        dst[...] = src[...].astype(_BF16)

    si = pl.program_id(1)
    nsub = TM1 // SUB1
    row = lax.broadcasted_iota(jnp.int32, (CHUNK, CHUNK), 0)
    col = lax.broadcasted_iota(jnp.int32, (CHUNK, CHUNK), 1)
    causal = col <= row
    w_mix = [jnp.where(causal, sw_ref[hd], 0.0).astype(_BF16) for hd in range(N_HEADS)]
    w_mix = [jnp.concatenate(w_mix[h0:h0 + MIX_HEADS], axis=0)
             for h0 in range(0, N_HEADS, MIX_HEADS)]

    def project(t):
        x = x_ref[0, t * SUB1:(t + 1) * SUB1, :]
        ms = jnp.mean(x * x, axis=-1, keepdims=True)
        h = (x * lax.rsqrt(ms + EPS) * g1_ref[...]).astype(_BF16)
        return [lax.dot_general(win_ref[i * WIDTH:(i + 1) * WIDTH, :], h, _NT, preferred_element_type=_F32)
                for i in range(5)]

    pending = []
    for t in range(nsub):
        pending.append((t, project(t)))
        if len(pending) > 1:
            _inproj_finish(*pending.pop(0), si * nsub, w_mix, cols_ref, sb_ref,
                           ya_ref, qT_ref, k_ref, vT_ref, km_ref)
    _inproj_finish(*pending.pop(0), si * nsub, w_mix, cols_ref, sb_ref,
                   ya_ref, qT_ref, k_ref, vT_ref, km_ref)


def _inproj_finish(t, projs, blk0, w_mix, cols_ref, sb_ref, ya_ref, qT_ref, k_ref, vT_ref, km_ref):
    pu, pv, qt, kt, vbt = projs
    nchunk = SUB1 // CHUNK
    rows = slice(t * SUB1, (t + 1) * SUB1)

    u = jax.nn.gelu(pu)
    v = jax.nn.gelu(pv)
    ya = []
    per_head = nchunk * HEAD_DIM
    for h0 in range(0, N_HEADS, MIX_HEADS):
        heads = range(h0, h0 + MIX_HEADS)
        lhs = []
        for hd in heads:
            sl = slice(hd * HEAD_DIM, (hd + 1) * HEAD_DIM)
            vh = v[sl, :]
            mu = jnp.mean(vh, axis=0, keepdims=True)
            vc = vh - mu
            var = jnp.mean(vc * vc, axis=0, keepdims=True)
            vn = (vc * lax.rsqrt(var + EPS) * cols_ref[sl, COL_LN_G:COL_LN_G + 1]
                  + cols_ref[sl, COL_LN_B:COL_LN_B + 1])
            lhs += [vn[:, c * CHUNK:(c + 1) * CHUNK] for c in range(nchunk)]
        lhs = jnp.concatenate(lhs, axis=0).astype(_BF16)
        mixed_all = lax.dot_general(lhs, w_mix[h0 // MIX_HEADS], _NT, preferred_element_type=_F32)
        for i, hd in enumerate(heads):
            sl = slice(hd * HEAD_DIM, (hd + 1) * HEAD_DIM)
            mixed = mixed_all[i * per_head:(i + 1) * per_head, i * CHUNK:(i + 1) * CHUNK]
            mixed = mixed + sb_ref[hd:hd + 1, :]
            mixed = jnp.concatenate(
                [mixed[c * HEAD_DIM:(c + 1) * HEAD_DIM, :] for c in range(nchunk)], axis=1)
            ya.append(_head_rms(u[sl, :] * mixed, cols_ref[sl, COL_A_G:COL_A_G + 1]))
    ya_ref[0, rows, :] = jnp.concatenate(ya, axis=0).T.astype(_BF16)

    qn = jnp.concatenate(
        [_head_rms(qt[hd * HEAD_DIM:(hd + 1) * HEAD_DIM, :], cols_ref[0:HEAD_DIM, COL_Q_G:COL_Q_G + 1])
         for hd in range(N_HEADS)], axis=0) * Q_SCALE
    kn = jnp.concatenate(
        [_head_rms(kt[hd * HEAD_DIM:(hd + 1) * HEAD_DIM, :], cols_ref[0:HEAD_DIM, COL_K_G:COL_K_G + 1])
         for hd in range(N_HEADS)], axis=0)
    k_tok = kn.T
    qT_ref[0, :, rows] = qn.astype(_BF16)
    vT_ref[0, :, rows] = vbt.astype(_BF16)
    k_ref[0, rows, :] = k_tok.astype(_BF16)
    for b in range(SUB1 // BLK):
        kb = k_tok[b * BLK:(b + 1) * BLK, :]
        km_ref[0, pl.ds((blk0 + t) * (SUB1 // BLK) + b, 1), :] = jnp.mean(kb, axis=0, keepdims=True)


def _attn_jobs(qT_ref, k_ref, vT_ref, km_ref, cols_ref, kx_scr, store_tile):
    nb = k_ref.shape[1] // BLK
    frow = lax.broadcasted_iota(jnp.int32, (PAIR, BLK), 0)
    key_pos = lax.broadcasted_iota(jnp.int32, (BLK, BLK), 0)
    qry_pos = lax.broadcasted_iota(jnp.int32, (BLK, BLK), 1)
    causal_bias = jnp.where(key_pos <= qry_pos, 0.0, -jnp.inf)

    km = km_ref[0]
    km_hi = km.astype(_BF16)
    km_lo = (km - km_hi.astype(_F32)).astype(_BF16)
    kx_scr[0:2 * nb, :] = jnp.concatenate([km_hi, km_lo], axis=0)
    kx_scr[2 * nb:, :] = k_ref[0]

    def scores(jj, hh):
        nkeys = (jj + 1) * BLK
        qT = qT_ref[0, :, jj * BLK:(jj + 1) * BLK]
        in_head = (frow >= hh * HEAD_DIM) & (frow < (hh + 1) * HEAD_DIM)
        qm = jnp.where(in_head, qT, jnp.zeros_like(qT))
        bias = None
        if jj > TOPK:
            sx = jnp.dot(kx_scr[0:2 * nb + nkeys, :], qm, preferred_element_type=_F32)
            gate = sx[0:jj, :] + sx[nb:nb + jj, :]
            s = sx[2 * nb:, :]
            blk_id = lax.broadcasted_iota(jnp.int32, (jj, BLK), 0)
            rank = jnp.zeros((jj, BLK), _F32)
            for m in range(jj):
                gm = gate[m:m + 1, :]
                ahead = (gm > gate) | ((gm == gate) & (blk_id > m))
                rank = rank + jnp.where(ahead, 1.0, 0.0)
            bias = jnp.where(rank < TOPK, 0.0, -jnp.inf)
        else:
            s = jnp.dot(kx_scr[2 * nb:2 * nb + nkeys, :], qm, preferred_element_type=_F32)
        blocks, m = [], None
        for n in range(jj + 1):
            sn = s[n * BLK:(n + 1) * BLK, :]
            if n == jj:
                sn = sn + causal_bias
            elif bias is not None:
                sn = sn + bias[n:n + 1, :]
            blocks.append(sn)
            mn = jnp.max(sn, axis=0, keepdims=True)
            m = mn if m is None else jnp.maximum(m, mn)
        return blocks, m

    tiles = {}

    def softmax(jj, hh, blocks, m):
        probs, l = [], None
        for sn in blocks:
            p = jnp.exp2(sn - m)
            ln = jnp.sum(p, axis=0, keepdims=True)
            l = ln if l is None else l + ln
            probs.append(p.astype(_BF16))
        p = jnp.concatenate(probs, axis=0) if jj > 0 else probs[0]
        return jj, hh, p, l

    def pv(jj, hh, p, l):
        vrows = slice(hh * HEAD_DIM, (hh + 1) * HEAD_DIM)
        o = jnp.dot(vT_ref[0, vrows, 0:(jj + 1) * BLK], p, preferred_element_type=_F32)
        tiles.setdefault(jj, []).append(
            _head_rms(o / l, cols_ref[vrows, COL_B_G:COL_B_G + 1]))
        if len(tiles[jj]) == HEADS_PER_STEP:
            y = jnp.concatenate(tiles.pop(jj), axis=0)
            store_tile(jj, y.T.astype(_BF16))

    jobs = [(jj, hh) for jj in range(nb) for hh in range(HEADS_PER_STEP)]
    return jobs, scores, softmax, pv


def _attn_kernel(qT_ref, k_ref, vT_ref, km_ref, cols_ref, yb_ref, kx_scr):
    def store_tile(jj, tile):
        yb_ref[0, jj * BLK:(jj + 1) * BLK, :] = tile

    jobs, scores, softmax, pv = _attn_jobs(qT_ref, k_ref, vT_ref, km_ref, cols_ref, kx_scr, store_tile)
    pending = []
    for job in jobs:
        pending.append((*job, *scores(*job)))
        if len(pending) > SCORES_AHEAD:
            pv(*softmax(*pending.pop(0)))
    for job in pending:
        pv(*softmax(*job))


def _ffn_stages(load_x, load_y, wo_ref, g2_ref, wg_ref, wup_ref, wd_ref, store):
    st = {}

    def outproj():
        st["x1"] = load_x() + jnp.dot(load_y(), wo_ref[...], preferred_element_type=_F32)

    def norm():
        x1 = st["x1"]
        ms = jnp.mean(x1 * x1, axis=-1, keepdims=True)
        st["h"] = (x1 * lax.rsqrt(ms + EPS) * g2_ref[...]).astype(_BF16)

    def gate():
        st["g"] = jnp.dot(st["h"], wg_ref[...], preferred_element_type=_F32)

    def up():
        st["u"] = jnp.dot(st.pop("h"), wup_ref[...], preferred_element_type=_F32)

    def act():
        st["a"] = (jax.nn.silu(st.pop("g")) * st.pop("u")).astype(_BF16)

    def down():
        store(st.pop("x1") + jnp.dot(st.pop("a"), wd_ref[...], preferred_element_type=_F32))

    return outproj, norm, gate, up, act, down


def _ffn_kernel(x_ref, ya_ref, yb_ref, wo_hbm, g2_ref, wg_hbm, wup_hbm, wd_hbm, _aliased_out, o_ref,
                wo_ref, wg_ref, wup_ref, wd_ref, sems):
    first = pl.program_id(0) == 0
    copies = [pltpu.make_async_copy(src, dst, sems.at[i]) for i, (src, dst) in enumerate(
        ((wo_hbm, wo_ref), (wg_hbm, wg_ref), (wup_hbm, wup_ref), (wd_hbm, wd_ref)))]

    @pl.when(first)
    def _():
        for copy in copies:
            copy.start()

    def arrived(i):
        pl.when(first)(copies[i].wait)

    def store(val):
        o_ref[0] = val

    outproj, norm, gate, up, act, down = _ffn_stages(
        lambda: x_ref[0], lambda: jnp.concatenate([ya_ref[0], yb_ref[0]], axis=1),
        wo_ref, g2_ref, wg_ref, wup_ref, wd_ref, store)
    arrived(0)
    outproj()
    norm()
    arrived(1)
    gate()
    arrived(2)
    up()
    act()
    arrived(3)
    down()


def _attn_ffn_kernel(qT_ref, k_ref, vT_ref, km_ref, cols_ref,
                     x_ref, ya_ref, yb0_ref, wo_ref, g2_ref, wg_ref, wup_ref, wd_ref,
                     yb_ref, o_ref, kx_scr, yb_scr):
    r, p = pl.program_id(0), pl.program_id(1)
    n_pair = yb_scr.shape[1]

    @pl.when((r == 0) & (p == 0))
    def _():
        for q in range(n_pair):
            yb_scr[0, q] = yb0_ref[0, :, q * PAIR:(q + 1) * PAIR]

    rd = r % 2
    wr = 1 - rd

    def store_tile(jj, tile):
        yb_ref[0, jj * BLK:(jj + 1) * BLK, :] = tile
        yb_scr[wr, p, jj * BLK:(jj + 1) * BLK, :] = tile

    jobs, scores, softmax, pv = _attn_jobs(qT_ref, k_ref, vT_ref, km_ref, cols_ref, kx_scr, store_tile)

    subs = []
    for t in range(TM3 // SUB3):
        rows = slice(t * SUB3, (t + 1) * SUB3)

        def load_y(t=t, rows=rows):
            tok = pl.ds(pl.multiple_of(p * TM3 + t * SUB3, SUB3), SUB3)
            return jnp.concatenate(
                [ya_ref[0, rows, :]] + [yb_scr[rd, q, tok, :] for q in range(n_pair)], axis=1)

        def store(val, rows=rows):
            o_ref[0, rows, :] = val

        subs.append(_ffn_stages(lambda rows=rows: x_ref[0, rows, :], load_y,
                                wo_ref, g2_ref, wg_ref, wup_ref, wd_ref, store))
    (a_out, a_norm, a_gate, a_up, a_act, a_down), (b_out, b_norm, b_gate, b_up, b_act, b_down) = subs
    slots = [(a_out,), (b_out, a_norm), (a_gate, b_norm), (a_up, a_act),
             (b_gate,), (b_up, b_act), (a_down,), (b_down,)]
    per_slot = len(jobs) // len(slots)
    assert per_slot * len(slots) == len(jobs)
    for i, slot in enumerate(slots):
        mine = jobs[i * per_slot:(i + 1) * per_slot]
        started = [(*job, *scores(*job)) for job in mine]
        weighted = [softmax(*job) for job in started]
        for stage in slot:
            stage()
        for job in weighted:
            pv(*job)


def _full(shape):
    return pl.BlockSpec(shape, lambda *_: (0,) * len(shape), pipeline_mode=pl.Buffered(1))


def _layer(x, norm1_g, w_in, sgu_ln_g, sgu_ln_b, sgu_w, sgu_b, q_norm_g, k_norm_g,
           out_norm_a_g, out_norm_b_g, w_out, norm2_g, w_gate, w_up, w_down):
    B, S, D = x.shape
    nb = S // BLK
    assert D == D_MODEL and S % TM1 == 0

    w_in_t = w_in.astype(_BF16).T
    cols = [None] * N_COLS
    cols[COL_LN_G], cols[COL_LN_B] = sgu_ln_g, sgu_ln_b
    cols[COL_A_G], cols[COL_B_G] = out_norm_a_g, out_norm_b_g
    cols[COL_Q_G], cols[COL_K_G] = jnp.tile(q_norm_g, N_HEADS), jnp.tile(k_norm_g, N_HEADS)
    cols = jnp.stack(cols, axis=1).astype(_F32)

    steps1 = B * (S // TM1)
    ffn_w = (w_out, w_gate, w_up, w_down)
    assert all(w.shape[0] % (16 * steps1) == 0 for w in ffn_w)
    slab_specs = [pl.BlockSpec((w.shape[0] // steps1, w.shape[1]), lambda b, s: (b * (S // TM1) + s, 0))
                  for w in ffn_w]

    ya, qT, k, vT, km, wo_b, wg_b, wup_b, wd_b = pl.pallas_call(
        _inproj_kernel,
        grid=(B, S // TM1),
        in_specs=[
            pl.BlockSpec((1, TM1, D), lambda b, s: (b, s, 0)),
            _full((1, D)),
            _full(w_in_t.shape),
            _full((WIDTH, N_COLS)),
            _full((N_HEADS, CHUNK, CHUNK)), _full((N_HEADS, CHUNK)),
            *slab_specs,
        ],
        out_specs=[
            pl.BlockSpec((1, TM1, WIDTH), lambda b, s: (b, s, 0)),
            pl.BlockSpec((1, WIDTH, TM1), lambda b, s: (b, 0, s)),
            pl.BlockSpec((1, TM1, WIDTH), lambda b, s: (b, s, 0)),
            pl.BlockSpec((1, WIDTH, TM1), lambda b, s: (b, 0, s)),
            pl.BlockSpec((1, nb, WIDTH), lambda b, s: (b, 0, 0)),
            *slab_specs,
        ],
        out_shape=[
            jax.ShapeDtypeStruct((B, S, WIDTH), _BF16),
            jax.ShapeDtypeStruct((B, WIDTH, S), _BF16),
            jax.ShapeDtypeStruct((B, S, WIDTH), _BF16),
            jax.ShapeDtypeStruct((B, WIDTH, S), _BF16),
            jax.ShapeDtypeStruct((B, nb, WIDTH), _F32),
            *[jax.ShapeDtypeStruct(w.shape, _BF16) for w in ffn_w],
        ],
        compiler_params=pltpu.CompilerParams(
            dimension_semantics=("parallel", "arbitrary"),
            vmem_limit_bytes=VMEM_LIMIT_BYTES),
        name="inproj_sgu",
    )(x, norm1_g.reshape(1, D), w_in_t, cols, sgu_w, sgu_b, *ffn_w)

    n_pair = WIDTH // PAIR
    assert S == n_pair * TM3 and B >= 2

    def attn_specs(row):
        return [
            pl.BlockSpec((1, PAIR, S), lambda r, p: (row(r), p, 0)),
            pl.BlockSpec((1, S, PAIR), lambda r, p: (row(r), 0, p)),
            pl.BlockSpec((1, PAIR, S), lambda r, p: (row(r), p, 0)),
            pl.BlockSpec((1, nb, PAIR), lambda r, p: (row(r), 0, p)),
            pl.BlockSpec((PAIR, N_COLS), lambda r, p: (p, 0)),
        ]

    kx_scratch = pltpu.VMEM((2 * nb + S, PAIR), _BF16)
    ffn_w_specs = [_full((2 * WIDTH, D)), _full((1, D)), _full((D, D_FF)), _full((D, D_FF)), _full((D_FF, D))]
    ffn_w_args = (wo_b, norm2_g.reshape(1, D), wg_b, wup_b, wd_b)

    yb0 = pl.pallas_call(
        _attn_kernel,
        grid=(1, n_pair),
        in_specs=attn_specs(lambda r: r),
        out_specs=pl.BlockSpec((1, S, PAIR), lambda r, p: (r, 0, p)),
        out_shape=jax.ShapeDtypeStruct((1, S, WIDTH), _BF16),
        scratch_shapes=[kx_scratch],
        compiler_params=pltpu.CompilerParams(
            dimension_semantics=("parallel", "parallel"),
            vmem_limit_bytes=VMEM_LIMIT_BYTES),
        name="moba_attn",
    )(qT, k, vT, km, cols)

    yb, out = pl.pallas_call(
        _attn_ffn_kernel,
        grid=(B - 1, n_pair),
        in_specs=[
            *attn_specs(lambda r: r + 1),
            pl.BlockSpec((1, TM3, D), lambda r, p: (r, p, 0)),
            pl.BlockSpec((1, TM3, WIDTH), lambda r, p: (r, p, 0)),
            _full((1, S, WIDTH)),
            *ffn_w_specs,
        ],
        out_specs=[
            pl.BlockSpec((1, S, PAIR), lambda r, p: (r, 0, p)),
            pl.BlockSpec((1, TM3, D), lambda r, p: (r, p, 0)),
        ],
        out_shape=[
            jax.ShapeDtypeStruct((B - 1, S, WIDTH), _BF16),
            jax.ShapeDtypeStruct((B, S, D), _F32),
        ],
        scratch_shapes=[kx_scratch, pltpu.VMEM((2, n_pair, S, PAIR), _BF16)],
        compiler_params=pltpu.CompilerParams(
            dimension_semantics=("arbitrary", "arbitrary"),
            vmem_limit_bytes=VMEM_LIMIT_BYTES),
        name="attn_ffn",
    )(qT, k, vT, km, cols, x, ya, yb0, *ffn_w_args)

    last = B - 1
    in_hbm = pl.BlockSpec(memory_space=pl.ANY)
    out = pl.pallas_call(
        _ffn_kernel,
        grid=(n_pair,),
        in_specs=[
            pl.BlockSpec((1, TM3, D), lambda i: (last, i, 0)),
            pl.BlockSpec((1, TM3, WIDTH), lambda i: (last, i, 0)),
            pl.BlockSpec((1, TM3, WIDTH), lambda i: (last - 1, i, 0)),
            in_hbm, _full((1, D)), in_hbm, in_hbm, in_hbm, in_hbm,
        ],
        out_specs=pl.BlockSpec((1, TM3, D), lambda i: (last, i, 0)),
        out_shape=jax.ShapeDtypeStruct((B, S, D), _F32),
        input_output_aliases={8: 0},
        scratch_shapes=[*[pltpu.VMEM(w.shape, _BF16) for w in (wo_b, wg_b, wup_b, wd_b)],
                        pltpu.SemaphoreType.DMA((4,))],
        compiler_params=pltpu.CompilerParams(
            dimension_semantics=("arbitrary",),
            vmem_limit_bytes=VMEM_LIMIT_BYTES),
        name="outproj_ffn",
    )(x, ya, yb, *ffn_w_args, out)
    return out


def kernel(x, norm1_g, w_in, sgu_ln_g, sgu_ln_b, sgu_w, sgu_b, q_norm_g, k_norm_g,
           out_norm_a_g, out_norm_b_g, w_out, norm2_g, w_gate, w_up, w_down):
    depth = norm1_g.shape[0]
    for l in range(depth):
        x = _layer(x, norm1_g[l], w_in[l], sgu_ln_g[l], sgu_ln_b[l], sgu_w[l], sgu_b[l],
                   q_norm_g[l], k_norm_g[l], out_norm_a_g[l], out_norm_b_g[l], w_out[l],
                   norm2_g[l], w_gate[l], w_up[l], w_down[l])
    return x
```
